```python
import jax, jax.numpy as jnp
from jax import lax
import numpy as np

D_MODEL = 1024
BATCH = 16
SEQ = 2048
DEPTH = 2

CTX_LEN = 256
GRID_W = 64
ROPE_BASE = 10000.0
NORM_EPS = 1e-6
LN_EPS = 1e-5
NEG_INF = -1e30
N_MOD = 9
FFN_RES = 0.5
D_FF = 2816

A_HEADS = 8
A_KV_HEADS = 2
A_HEAD_DIM = 64
A_WINDOW = 128
A_BLOCK = 128
B_CH = 512
B_KERNEL = 31
A_Q_W = A_HEADS * A_HEAD_DIM
A_KV_W = A_KV_HEADS * A_HEAD_DIM
AB_IN_W = A_Q_W + 2 * A_KV_W + 2 * B_CH
AB_OUT_W = A_Q_W + B_CH

C_HEADS = 8
C_Q_LORA = 256
C_KV_LORA = 256
C_NOPE = 128
C_ROPE = 64
C_V = 128
C_BLOCK = 128

kernel_name = "hybrid_prefix_dit_macaron_block"


def rms_norm(x, g):
    xf = x.astype(jnp.float32)
    y = xf * lax.rsqrt(jnp.mean(xf * xf, axis=-1, keepdims=True) + NORM_EPS)
    return (y * g.astype(jnp.float32)).astype(x.dtype)


def layer_norm(x, g, b):
    xf = x.astype(jnp.float32)
    mu = jnp.mean(xf, axis=-1, keepdims=True)
    xc = xf - mu
    var = jnp.mean(xc * xc, axis=-1, keepdims=True)
    y = xc * lax.rsqrt(var + LN_EPS) * g.astype(jnp.float32) + b.astype(jnp.float32)
    return y.astype(x.dtype)


def modulate(xn, shift, scale):
    return xn * (1 + scale) + shift


def axial_rope_tables(rows, d_rot):
    t = jnp.arange(rows * GRID_W)
    row = (t // GRID_W).astype(jnp.float32)
    col = (t % GRID_W).astype(jnp.float32)
    d_axis = d_rot // 2
    inv_freq = ROPE_BASE ** (-jnp.arange(0, d_axis, 2, dtype=jnp.float32) / d_axis)
    ang_r = row[:, None] * inv_freq
    ang_c = col[:, None] * inv_freq
    ang = jnp.concatenate([ang_r, ang_r, ang_c, ang_c], axis=-1)
    return jnp.cos(ang), jnp.sin(ang)


def apply_rope(x, cos, sin):
    d = x.shape[-1]
    xr = x.reshape(x.shape[:-1] + (2, 2, d // 4))
    rot = jnp.stack([-xr[..., 1, :], xr[..., 0, :]], axis=-2).reshape(x.shape)
    cos = cos[:, None, :].astype(x.dtype)
    sin = sin[:, None, :].astype(x.dtype)
    return x * cos + rot * sin


def swiglu(h, w_in, w_out):
    g, u = jnp.split(h @ w_in, 2, axis=-1)
    return (jax.nn.silu(g) * u) @ w_out


def ffn_sublayer(x, g, shift, scale, gate, w_in, w_out):
    return x + FFN_RES * gate * swiglu(modulate(rms_norm(x, g), shift, scale), w_in, w_out)


def softmax_with_sink(s, sink):
    s_all = jnp.concatenate([s, jnp.broadcast_to(sink, s.shape[:-1] + (1,))], axis=-1)
    return jax.nn.softmax(s_all, axis=-1)[..., :-1]


def attend(q, k, v, ok, sink, scale):
    s = jnp.einsum('bqhgd,bkhd->bhgqk', q, k).astype(jnp.float32) * scale
    if ok is not None:
        s = jnp.where(ok, s, NEG_INF)
    if sink is not None:
        p = softmax_with_sink(s, sink[None, :, :, None, None])
    else:
        p = jax.nn.softmax(s, axis=-1)
    o = jnp.einsum('bhgqk,bkhd->bqhgd', p.astype(v.dtype), v)
    return o.reshape(o.shape[:2] + (-1,))


def window_gqa(q, k, v, kc, vc, sink, scale):
    B, S = q.shape[:2]
    nb = S // A_BLOCK
    L = kc.shape[1]
    pad = ((0, 0), (A_BLOCK, A_BLOCK), (0, 0), (0, 0))
    k_pad = jnp.pad(k, pad)
    v_pad = jnp.pad(v, pad)
    ctx_ok = jnp.ones((A_BLOCK, L), dtype=bool)

    def block(n):
        start = n * A_BLOCK
        q_n = lax.dynamic_slice_in_dim(q, start, A_BLOCK, axis=1)
        k_n = jnp.concatenate([lax.dynamic_slice_in_dim(k_pad, start, 3 * A_BLOCK, axis=1), kc], axis=1)
        v_n = jnp.concatenate([lax.dynamic_slice_in_dim(v_pad, start, 3 * A_BLOCK, axis=1), vc], axis=1)
        q_pos = start + jnp.arange(A_BLOCK)
        k_pos = start - A_BLOCK + jnp.arange(3 * A_BLOCK)
        near = ((jnp.abs(q_pos[:, None] - k_pos[None, :]) <= A_WINDOW)
                & (k_pos >= 0)[None, :] & (k_pos < S)[None, :])
        ok = jnp.concatenate([near, ctx_ok], axis=1)
        return attend(q_n, k_n, v_n, ok, sink, scale)

    o = lax.map(block, jnp.arange(nb))
    return jnp.moveaxis(o, 0, 1).reshape(B, S, -1)


def conformer_conv(u, w_dw, b_dw, ln_g, ln_b):
    a, g = jnp.split(u, 2, axis=-1)
    y = a * jax.nn.sigmoid(g)
    y = lax.conv_general_dilated(
        y, w_dw[:, None, :].astype(y.dtype), window_strides=(1,),
        padding=[(B_KERNEL // 2, B_KERNEL // 2)],
        dimension_numbers=('NWC', 'WIO', 'NWC'),
        feature_group_count=y.shape[-1]) + b_dw
    return jax.nn.silu(layer_norm(y, ln_g, ln_b))


def ab_mixer(xm, hm, w_in, sink, w_dw, b_dw, ln_g, ln_b, w_out, rope, ctx_out):
    B, S, _ = xm.shape
    L = hm.shape[1]
    G = A_HEADS // A_KV_HEADS
    sink = sink.astype(jnp.float32).reshape(A_KV_HEADS, G)
    scale = A_HEAD_DIM ** -0.5
    i_k = A_Q_W
    i_v = i_k + A_KV_W
    i_u = i_v + A_KV_W
    px = xm @ w_in
    ph = hm @ w_in
    qx = apply_rope(px[..., :i_k].reshape(B, S, A_HEADS, A_HEAD_DIM), *rope)
    qx = qx.reshape(B, S, A_KV_HEADS, G, A_HEAD_DIM)
    kx = apply_rope(px[..., i_k:i_v].reshape(B, S, A_KV_HEADS, A_HEAD_DIM), *rope)
    vx = px[..., i_v:i_u].reshape(B, S, A_KV_HEADS, A_HEAD_DIM)
    kh = ph[..., i_k:i_v].reshape(B, L, A_KV_HEADS, A_HEAD_DIM)
    vh = ph[..., i_v:i_u].reshape(B, L, A_KV_HEADS, A_HEAD_DIM)
    ax = window_gqa(qx, kx, vx, kh, vh, sink, scale)
    bx = conformer_conv(px[..., i_u:], w_dw, b_dw, ln_g, ln_b)
    ox = jnp.concatenate([ax, bx], axis=-1) @ w_out
    if not ctx_out:
        return ox, None
    qh = ph[..., :i_k].reshape(B, L, A_KV_HEADS, G, A_HEAD_DIM)
    ah = attend(qh, kh, vh, None, sink, scale)
    bh = conformer_conv(ph[..., i_u:], w_dw, b_dw, ln_g, ln_b)
    oh = jnp.concatenate([ah, bh], axis=-1) @ w_out
    return ox, oh


def mla_q(h, w_dq, g_q, w_uq, rope):
    B, N, _ = h.shape
    cq = rms_norm(h @ w_dq, g_q)
    q = (cq @ w_uq).reshape(B, N, C_HEADS, C_NOPE + C_ROPE)
    q_nope, q_rope = q[..., :C_NOPE], q[..., C_NOPE:]
    if rope is not None:
        q_rope = apply_rope(q_rope, *rope)
    return jnp.concatenate([q_nope, q_rope], axis=-1)


def mla_kv(h, w_dkv, g_kv, w_uk, w_uv, rope):
    B, N, _ = h.shape
    ckv_kr = h @ w_dkv
    ckv = rms_norm(ckv_kr[..., :C_KV_LORA], g_kv)
    k_rope = ckv_kr[..., C_KV_LORA:][:, :, None, :]
    if rope is not None:
        k_rope = apply_rope(k_rope, *rope)
    k_nope = (ckv @ w_uk).reshape(B, N, C_HEADS, C_NOPE)
    v = (ckv @ w_uv).reshape(B, N, C_HEADS, C_V)
    k = jnp.concatenate([k_nope, jnp.broadcast_to(k_rope, (B, N, C_HEADS, C_ROPE))], axis=-1)
    return k, v


def mla_mixer(xm, hm, w_dq, g_q, w_uq, w_dkv, g_kv, w_uk, w_uv, w_o, rope, ctx_out):
    B, S, _ = xm.shape
    nb = S // C_BLOCK
    scale = (C_NOPE + C_ROPE) ** -0.5
    qx = mla_q(xm, w_dq, g_q, w_uq, rope)
    kx, vx = mla_kv(xm, w_dkv, g_kv, w_uk, w_uv, rope)
    kh, vh = mla_kv(hm, w_dkv, g_kv, w_uk, w_uv, None)
    k_all = jnp.concatenate([kx, kh], axis=1)
    v_all = jnp.concatenate([vx, vh], axis=1)

    def block(n):
        q_n = lax.dynamic_slice_in_dim(qx, n * C_BLOCK, C_BLOCK, axis=1)[:, :, :, None, :]
        return attend(q_n, k_all, v_all, None, None, scale)

    o = lax.map(block, jnp.arange(nb))
    ox = jnp.moveaxis(o, 0, 1).reshape(B, S, -1) @ w_o
    if not ctx_out:
        return ox, None
    qh = mla_q(hm, w_dq, g_q, w_uq, None)
    oh = attend(qh[:, :, :, None, :], kh, vh, None, None, scale) @ w_o
    return ox, oh


def setup_inputs(seed: int = 0) -> dict:
    key = jax.random.key(seed)
    ks = jax.random.split(key, 32)
    f32 = jnp.float32
    n_even = (DEPTH + 1) // 2
    n_odd = DEPTH // 2

    def w(k, shape, fan_in, gain=1.0):
        return jax.random.normal(k, shape, f32) * (gain * fan_in ** -0.5)

    def ones_noise(k, shape):
        return 1.0 + 0.02 * jax.random.normal(k, shape, f32)

    def small(k, shape):
        return 0.02 * jax.random.normal(k, shape, f32)

    return {
        "x": jax.random.normal(ks[0], (BATCH, SEQ, D_MODEL), f32),
        "c": jax.random.normal(ks[1], (BATCH, D_MODEL), f32),
        "ctx": jax.random.normal(ks[2], (BATCH, CTX_LEN, D_MODEL), f32),
        "c_ctx": jax.random.normal(ks[3], (D_MODEL,), f32),
        "w_mod": w(ks[4], (DEPTH, D_MODEL, N_MOD * D_MODEL), D_MODEL, 0.5),
        "b_mod": small(ks[5], (DEPTH, N_MOD * D_MODEL)),
        "g_norm": ones_noise(ks[6], (DEPTH, 3, D_MODEL)),
        "ffn_w_in": w(ks[7], (DEPTH, 2, D_MODEL, 2 * D_FF), D_MODEL),
        "ffn_w_out": w(ks[8], (DEPTH, 2, D_FF, D_MODEL), D_FF),
        "ab_w_in": w(ks[9], (n_even, D_MODEL, AB_IN_W), D_MODEL),
        "a_sink": 0.5 * jax.random.normal(ks[10], (n_even, A_HEADS), f32),
        "b_w_dw": w(ks[11], (n_even, B_KERNEL, B_CH), B_KERNEL),
        "b_b_dw": small(ks[12], (n_even, B_CH)),
        "b_ln_g": ones_noise(ks[13], (n_even, B_CH)),
        "b_ln_b": small(ks[14], (n_even, B_CH)),
        "ab_w_out": w(ks[15], (n_even, AB_OUT_W, D_MODEL), AB_OUT_W),
        "c_w_dq": w(ks[16], (n_odd, D_MODEL, C_Q_LORA), D_MODEL),
        "c_g_q": ones_noise(ks[17], (n_odd, C_Q_LORA)),
        "c_w_uq": w(ks[18], (n_odd, C_Q_LORA, C_HEADS * (C_NOPE + C_ROPE)), C_Q_LORA),
        "c_w_dkv": w(ks[19], (n_odd, D_MODEL, C_KV_LORA + C_ROPE), D_MODEL),
        "c_g_kv": ones_noise(ks[20], (n_odd, C_KV_LORA)),
        "c_w_uk": w(ks[21], (n_odd, C_KV_LORA, C_HEADS * C_NOPE), C_KV_LORA),
        "c_w_uv": w(ks[22], (n_odd, C_KV_LORA, C_HEADS * C_V), C_KV_LORA),
        "c_w_o": w(ks[23], (n_odd, C_HEADS * C_V, D_MODEL), C_HEADS * C_V),
        "g_final": ones_noise(ks[24], (D_MODEL,)),
    }


def reference(x, c, ctx, c_ctx, w_mod, b_mod, g_norm, ffn_w_in, ffn_w_out,
              ab_w_in, a_sink, b_w_dw, b_b_dw, b_ln_g, b_ln_b, ab_w_out,
              c_w_dq, c_g_q, c_w_uq, c_w_dkv, c_g_kv, c_w_uk, c_w_uv, c_w_o, g_final):
    rows = x.shape[1] // GRID_W
    rope_a = axial_rope_tables(rows, A_HEAD_DIM)
    rope_c = axial_rope_tables(rows, C_ROPE)
    silu_c = jax.nn.silu(c)
    silu_cc = jax.nn.silu(c_ctx)
    h = ctx
    for i in range(DEPTH):
        last = i == DEPTH - 1
        j = i // 2
        mx = jnp.split((silu_c @ w_mod[i] + b_mod[i])[:, None, :], N_MOD, axis=-1)
        mc = jnp.split((silu_cc @ w_mod[i] + b_mod[i])[None, None, :], N_MOD, axis=-1)
        x = ffn_sublayer(x, g_norm[i, 0], mx[0], mx[1], mx[2], ffn_w_in[i, 0], ffn_w_out[i, 0])
        h = ffn_sublayer(h, g_norm[i, 0], mc[0], mc[1], mc[2], ffn_w_in[i, 0], ffn_w_out[i, 0])
        xm = modulate(rms_norm(x, g_norm[i, 1]), mx[3], mx[4])
        hm = modulate(rms_norm(h, g_norm[i, 1]), mc[3], mc[4])
        if i % 2 == 0:
            ox, oh = ab_mixer(xm, hm, ab_w_in[j], a_sink[j], b_w_dw[j], b_b_dw[j],
                              b_ln_g[j], b_ln_b[j], ab_w_out[j], rope_a, not last)
        else:
            ox, oh = mla_mixer(xm, hm, c_w_dq[j], c_g_q[j], c_w_uq[j], c_w_dkv[j], c_g_kv[j],
                               c_w_uk[j], c_w_uv[j], c_w_o[j], rope_c, not last)
        x = x + mx[5] * ox
        x = ffn_sublayer(x, g_norm[i, 2], mx[6], mx[7], mx[8], ffn_w_in[i, 1], ffn_w_out[i, 1])
        if not last:
            h = h + mc[5] * oh
            h = ffn_sublayer(h, g_norm[i, 2], mc[6], mc[7], mc[8], ffn_w_in[i, 1], ffn_w_out[i, 1])
    return rms_norm(x, g_final)
```

```python
import functools
import math

import jax
import jax.numpy as jnp
from jax import lax
from jax.experimental import pallas as pl
from jax.experimental.pallas import tpu as pltpu

F32 = jnp.float32
BF16 = jnp.bfloat16

GRID_W = 64
ROPE_BASE = 10000.0
NORM_EPS = 1e-6
LN_EPS = 1e-5
NEG_INF = -1e30
N_MOD = 9
FFN_RES = 0.5
A_HEADS = 8
A_KV_HEADS = 2
A_HEAD_DIM = 64
A_WINDOW = 128
A_BLOCK = 128
B_CH = 512
B_KERNEL = 31
C_HEADS = 8
C_Q_LORA = 256
C_KV_LORA = 256
C_NOPE = 128
C_ROPE = 64
C_V = 128

V7X_LANES = 128
V7X_MXU_DIM = 256
V7X_VMEM_BYTES = 64 * 1024 * 1024
V7X_VMEM_USABLE = 56 * 1024 * 1024

C_HEAD_PAD = V7X_MXU_DIM
CONV_HALO = 16


def _vmem_limit(nbytes):
    return int(min(V7X_VMEM_USABLE, max(16 * 1024 * 1024, nbytes * 3 // 2)))


def _params(nbytes, ngrid=1):
    return pltpu.CompilerParams(dimension_semantics=("arbitrary",) * ngrid,
                                vmem_limit_bytes=_vmem_limit(nbytes))


def _rms(x, g):
    return x * lax.rsqrt(jnp.mean(x * x, axis=-1, keepdims=True) + NORM_EPS) * g


def _resident(shape):
    nd = len(shape)
    return pl.BlockSpec(shape, lambda *_: (0,) * nd, pipeline_mode=pl.Buffered(1))


def _mod_kernel(c_ref, w_ref, b_ref, o_ref):
    c = c_ref[...]
    a = (c * jax.nn.sigmoid(c)).astype(BF16)
    o_ref[0] = jnp.dot(a, w_ref[0].astype(BF16), preferred_element_type=F32) + b_ref[0]


def _modulation(cc, w_mod, b_mod):
    depth, d, n = w_mod.shape
    r = cc.shape[0]
    tn = n // N_MOD
    nbytes = 2 * (d * tn * 4) + d * tn * 2 + 4 * r * (d + 2 * tn) * 4
    return pl.pallas_call(
        _mod_kernel,
        out_shape=jax.ShapeDtypeStruct((depth, r, n), F32),
        grid=(depth, n // tn),
        in_specs=[pl.BlockSpec((r, d), lambda l, j: (0, 0)),
                  pl.BlockSpec((1, d, tn), lambda l, j: (l, 0, j)),
                  pl.BlockSpec((1, 1, tn), lambda l, j: (l, 0, j))],
        out_specs=pl.BlockSpec((1, r, tn), lambda l, j: (l, 0, j)),
        compiler_params=_params(nbytes, 2),
        name="adaln_modulation",
    )(cc, w_mod, b_mod.reshape(depth, 1, n))


def _ffn_kernel(*refs, n_attn, sub, final_norm, n_chunks, ck):
    x_ref, mod_ref, g_ref = refs[:3]
    a_refs = refs[3:3 + n_attn]
    wo_refs = refs[3 + n_attn:3 + 2 * n_attn]
    win_ref, wout_ref = refs[3 + 2 * n_attn:5 + 2 * n_attn]
    gf_ref = refs[5 + 2 * n_attn] if final_norm else None
    o_ref = refs[-1]

    x = x_ref[...]
    mod = mod_ref[0]
    if n_attn:
        ox = jnp.dot(a_refs[0][...], wo_refs[0][...], preferred_element_type=F32)
        for a_ref, wo_ref in zip(a_refs[1:], wo_refs[1:]):
            ox = ox + jnp.dot(a_ref[...], wo_ref[...], preferred_element_type=F32)
        x = x + mod[5:6] * ox
    shift, scale, gate = mod[3 * sub:3 * sub + 1], mod[3 * sub + 1:3 * sub + 2], mod[3 * sub + 2:3 * sub + 3]
    h = (_rms(x, g_ref[sub:sub + 1]) * (1.0 + scale) + shift).astype(BF16)
    acc = None
    for c in range(n_chunks):
        gu = jnp.dot(h, win_ref[c], preferred_element_type=F32)
        g, u = gu[:, :ck], gu[:, ck:]
        act = (g * jax.nn.sigmoid(g) * u).astype(BF16)
        part = jnp.dot(act, wout_ref[c], preferred_element_type=F32)
        acc = part if acc is None else acc + part
    y = x + (FFN_RES * gate) * acc
    if final_norm:
        y = _rms(y, gf_ref[...])
    o_ref[...] = y


def _ffn(x, mod, g3, win, wout, *, rows, tm, seq, sub, attn=(), wo=(), g_final=None):
    d = x.shape[1]
    n_chunks, _, ck2 = win.shape
    ck = ck2 // 2
    nb = mod.shape[0] - 1
    n_attn = len(attn)
    row_spec = lambda w: pl.BlockSpec((tm, w), lambda i: (i, 0))
    in_specs = [row_spec(d),
                pl.BlockSpec((1, N_MOD, d), lambda i: (jnp.minimum(i * tm // seq, nb), 0, 0)),
                _resident(g3.shape)]
    in_specs += [row_spec(a.shape[1]) for a in attn]
    in_specs += [_resident(w.shape) for w in wo]
    in_specs += [_resident(win.shape), _resident(wout.shape)]
    args = [x, mod, g3, *attn, *wo, win, wout]
    if g_final is not None:
        in_specs.append(_resident((1, d)))
        args.append(g_final.reshape(1, d))
    weights = 2 * (win.size + wout.size + sum(w.size for w in wo))
    tiles = 2 * tm * (2 * d * 4 + sum(a.shape[1] for a in attn) * 2)
    temps = tm * (d * 4 * 3 + d * 2 + 2 * ck * 4 * 2 + ck * 2)
    kern = functools.partial(_ffn_kernel, n_attn=n_attn, sub=sub, final_norm=g_final is not None,
                             n_chunks=n_chunks, ck=ck)
    return pl.pallas_call(
        kern,
        out_shape=jax.ShapeDtypeStruct((rows, d), F32),
        grid=(rows // tm,),
        in_specs=in_specs,
        out_specs=row_spec(d),
        compiler_params=_params(weights + tiles + temps),
        name=f"ffn_sub{sub}" + ("_mix" if n_attn else "") + ("_final" if g_final is not None else ""),
    )(*args)


def _rope128(v, cos, sin_next, sin_prev):
    return (v * cos + pltpu.roll(v, V7X_LANES - 16, 1) * sin_next + pltpu.roll(v, 16, 1) * sin_prev)


def _proj0_kernel(x_ref, mod_ref, g_ref, w_ref, rope_ref, q_ref, kv_ref, y_ref, *, q_scale):
    mod = mod_ref[0]
    xm = (_rms(x_ref[...], g_ref[1:2]) * (1.0 + mod[4:5]) + mod[3:4]).astype(BF16)
    p = jnp.dot(xm, w_ref[...], preferred_element_type=F32)
    cos, s_next, s_prev = rope_ref[:, 0:128], rope_ref[:, 128:256], rope_ref[:, 256:384]
    nq = A_HEADS * A_HEAD_DIM // V7X_LANES
    for j in range(nq):
        blk = p[:, j * 128:(j + 1) * 128]
        q_ref[:, j * 128:(j + 1) * 128] = (_rope128(blk, cos, s_next, s_prev) * q_scale).astype(BF16)
    base = nq * 128
    for j in range(2):
        blk = p[:, base + j * 128:base + (j + 1) * 128]
        kv_ref[:, j * 128:(j + 1) * 128] = _rope128(blk, cos, s_next, s_prev).astype(BF16)
    kv_ref[:, 256:512] = p[:, base + 256:base + 512].astype(BF16)
    a = p[:, base + 512:base + 512 + B_CH]
    g = p[:, base + 512 + B_CH:base + 512 + 2 * B_CH]
    y_ref[...] = a * jax.nn.sigmoid(g)


def _proj0(x, mod, g3, w, rope_tab, *, tm, seq, n_lat_rows):
    rows, d = x.shape
    nb = mod.shape[0] - 1
    n_lat_tiles = n_lat_rows // tm
    tiles_per_seq = seq // tm
    rope_idx = lambda i: (jnp.where(i < n_lat_tiles, i % tiles_per_seq, tiles_per_seq), 0)
    nw = w.shape[1]
    nbytes = 2 * w.size + 2 * tm * (d * 4 + 384 * 4 + 1024 * 2 + 512 * 4) + tm * (nw * 4 + d * 6)
    return pl.pallas_call(
        functools.partial(_proj0_kernel, q_scale=A_HEAD_DIM ** -0.5),
        out_shape=(jax.ShapeDtypeStruct((rows, 512), BF16),
                   jax.ShapeDtypeStruct((rows, 512), BF16),
                   jax.ShapeDtypeStruct((rows, B_CH), F32)),
        grid=(rows // tm,),
        in_specs=[pl.BlockSpec((tm, d), lambda i: (i, 0)),
                  pl.BlockSpec((1, N_MOD, d), lambda i: (jnp.minimum(i * tm // seq, nb), 0, 0)),
                  _resident(g3.shape), _resident(w.shape),
                  pl.BlockSpec((tm, 384), rope_idx)],
        out_specs=(pl.BlockSpec((tm, 512), lambda i: (i, 0)),
                   pl.BlockSpec((tm, 512), lambda i: (i, 0)),
                   pl.BlockSpec((tm, B_CH), lambda i: (i, 0))),
        compiler_params=_params(nbytes),
        name="ab_in_proj",
    )(x, mod, g3, w, rope_tab)


def _gqa_core(q_ref, kv_blocks, lane_ref, sink_ref, o_ref, mask):
    r = q_ref.shape[0]
    m_lo, m_hi = lane_ref[0:1, :], lane_ref[1:2, :]
    kv = jnp.concatenate(kv_blocks, axis=0) if len(kv_blocks) > 1 else kv_blocks[0]
    nk = kv.shape[0]
    k, k_sw, v, v_sw = kv[:, 0:128], kv[:, 128:256], kv[:, 256:384], kv[:, 384:512]
    f_lo, f_hi = m_lo.astype(F32), m_hi.astype(F32)
    for hk in range(A_KV_HEADS):
        k_lo, k_hi = (k * m_lo, k_sw * m_hi) if hk == 0 else (k_sw * m_lo, k * m_hi)
        v_lo, v_hi = (v * m_lo, v_sw * m_hi) if hk == 0 else (v_sw * m_lo, v * m_hi)
        qs = jnp.concatenate([q_ref[:, (2 * hk) * 128:(2 * hk + 1) * 128],
                              q_ref[:, (2 * hk + 1) * 128:(2 * hk + 2) * 128]], axis=0)
        kcat = jnp.concatenate([k_lo, k_hi], axis=0)
        vcat = jnp.concatenate([v_lo, v_hi], axis=0)
        s = lax.dot_general(qs, kcat, (((1,), (1,)), ((), ())), preferred_element_type=F32)
        ps, rinv = [], []
        for j in range(2):
            pj, rj = [], []
            for par in range(2):
                sj = s[j * r:(j + 1) * r, par * nk:(par + 1) * nk]
                if mask is not None:
                    sj = jnp.where(mask, sj, NEG_INF)
                sink = sink_ref[4 * hk + 2 * j + par]
                m = jnp.maximum(jnp.max(sj, axis=-1, keepdims=True), sink)
                e = jnp.exp(sj - m)
                l = jnp.sum(e, axis=-1, keepdims=True) + jnp.exp(sink - m)
                pj.append(e.astype(BF16))
                rj.append(1.0 / l)
            ps.append(jnp.concatenate(pj, axis=1))
            rinv.append(rj[0] * f_lo + rj[1] * f_hi)
        p = jnp.concatenate(ps, axis=0)
        o = jnp.dot(p, vcat, preferred_element_type=F32)
        for j in range(2):
            blk = 2 * hk + j
            o_ref[:, blk * 128:(blk + 1) * 128] = (o[j * r:(j + 1) * r] * rinv[j]).astype(BF16)


def _gqa_window_kernel(sink_ref, q_ref, kvp_ref, kvc_ref, kvn_ref, kvx_ref, lane_ref, o_ref, *, nblk):
    n = pl.program_id(1)
    r = q_ref.shape[0]
    ctx = kvx_ref.shape[0]
    nk = 3 * A_BLOCK + ctx
    i = lax.broadcasted_iota(jnp.int32, (r, nk), 0)
    j = lax.broadcasted_iota(jnp.int32, (r, nk), 1)
    lo = jnp.where(n > 0, 0, A_BLOCK)
    hi = jnp.where(n < nblk - 1, 3 * A_BLOCK, 2 * A_BLOCK)
    band = (j >= i) & (j <= i + 2 * A_WINDOW) & (j >= lo) & (j < hi)
    mask = band | (j >= 3 * A_BLOCK)
    _gqa_core(q_ref, [kvp_ref[...], kvc_ref[...], kvn_ref[...], kvx_ref[...]], lane_ref, sink_ref, o_ref, mask)


def _gqa_ctx_kernel(sink_ref, q_ref, kvx_ref, lane_ref, o_ref):
    _gqa_core(q_ref, [kvx_ref[...]], lane_ref, sink_ref, o_ref, None)


def _gqa(q, kv, sink, lane_masks, *, batch, seq, ctx):
    rows = q.shape[0]
    nblk = seq // A_BLOCK
    n_lat = batch * seq
    ctx_blk0 = n_lat // ctx
    smem = pl.BlockSpec(memory_space=pltpu.SMEM)
    lane_spec = lambda nd: pl.BlockSpec((2, 128), lambda *_: (0, 0))
    row_blk = lambda f: pl.BlockSpec((A_BLOCK, 512), f)
    nbytes = 2 * (5 * A_BLOCK + ctx) * 512 * 2 + 6 * (2 * A_BLOCK) * 2 * (3 * A_BLOCK + ctx) * 4
    o_lat = pl.pallas_call(
        functools.partial(_gqa_window_kernel, nblk=nblk),
        out_shape=jax.ShapeDtypeStruct((n_lat, 512), BF16),
        grid=(batch, nblk),
        in_specs=[smem,
                  row_blk(lambda b, n: (b * nblk + n, 0)),
                  row_blk(lambda b, n: (b * nblk + jnp.maximum(n - 1, 0), 0)),
                  row_blk(lambda b, n: (b * nblk + n, 0)),
                  row_blk(lambda b, n: (b * nblk + jnp.minimum(n + 1, nblk - 1), 0)),
                  pl.BlockSpec((ctx, 512), lambda b, n: (ctx_blk0 + b, 0)),
                  lane_spec(2)],
        out_specs=row_blk(lambda b, n: (b * nblk + n, 0)),
        compiler_params=_params(nbytes, 2),
        name="gqa_window",
    )(sink, q, kv, kv, kv, kv, lane_masks)
    nbytes_c = 2 * 3 * ctx * 512 * 2 + 6 * (2 * ctx) * (2 * ctx) * 4
    o_ctx = pl.pallas_call(
        _gqa_ctx_kernel,
        out_shape=jax.ShapeDtypeStruct((rows - n_lat, 512), BF16),
        grid=(batch,),
        in_specs=[smem,
                  pl.BlockSpec((ctx, 512), lambda b: (ctx_blk0 + b, 0)),
                  pl.BlockSpec((ctx, 512), lambda b: (ctx_blk0 + b, 0)),
                  lane_spec(1)],
        out_specs=pl.BlockSpec((ctx, 512), lambda b: (b, 0)),
        compiler_params=_params(nbytes_c, 1),
        name="gqa_context",
    )(sink, q, kv, lane_masks)
    return jnp.concatenate([o_lat, o_ctx], axis=0)


def _conv_kernel(yp_ref, yc_ref, yn_ref, w_ref, b_ref, g_ref, beta_ref, o_ref, ext_ref, acc_ref, *,
                 n_lat_tiles, lat_tiles_per_seq, ctx_tiles_per_seq):
    i = pl.program_id(0)
    tm, ch = yc_ref.shape
    is_lat = i < n_lat_tiles
    pos = jnp.where(is_lat, i % lat_tiles_per_seq, (i - n_lat_tiles) % ctx_tiles_per_seq)
    last_pos = jnp.where(is_lat, lat_tiles_per_seq - 1, ctx_tiles_per_seq - 1)
    keep_prev = (pos > 0).astype(F32)
    keep_next = (pos < last_pos).astype(F32)
    ext_ref[0:CONV_HALO, :] = yp_ref[...] * keep_prev
    ext_ref[CONV_HALO:CONV_HALO + tm, :] = yc_ref[...]
    ext_ref[CONV_HALO + tm:, :] = yn_ref[...] * keep_next
    off = CONV_HALO - B_KERNEL // 2
    for cb in range(ch // V7X_LANES):
        cs = slice(cb * V7X_LANES, (cb + 1) * V7X_LANES)
        acc = w_ref[0:1, cs] * ext_ref[off:off + tm, cs]
        for t in range(1, B_KERNEL):
            acc = acc + w_ref[t:t + 1, cs] * ext_ref[off + t:off + t + tm, cs]
        acc_ref[:, cs] = acc + b_ref[:, cs]
    y = acc_ref[...]
    mu = jnp.mean(y, axis=-1, keepdims=True)
    yc = y - mu
    var = jnp.mean(yc * yc, axis=-1, keepdims=True)
    z = yc * lax.rsqrt(var + LN_EPS) * g_ref[...] + beta_ref[...]
    o_ref[...] = (z * jax.nn.sigmoid(z)).astype(BF16)


def _conformer_conv(y, w_dw, b_dw, ln_g, ln_b, *, tm, seq, ctx, n_lat_rows):
    rows, ch = y.shape
    hb = tm // CONV_HALO
    n_halo_blocks = rows // CONV_HALO
    kern = functools.partial(_conv_kernel, n_lat_tiles=n_lat_rows // tm,
                             lat_tiles_per_seq=seq // tm, ctx_tiles_per_seq=ctx // tm)
    nbytes = 2 * (tm + 2 * CONV_HALO) * ch * 4 + 2 * tm * ch * 2 + (2 * tm + 2 * CONV_HALO) * ch * 4 + 6 * tm * ch * 4
    vec = lambda: pl.BlockSpec((1, ch), lambda i: (0, 0))
    return pl.pallas_call(
        kern,
        out_shape=jax.ShapeDtypeStruct((rows, ch), BF16),
        grid=(rows // tm,),
        in_specs=[pl.BlockSpec((CONV_HALO, ch), lambda i: (jnp.maximum(i * hb - 1, 0), 0)),
                  pl.BlockSpec((tm, ch), lambda i: (i, 0)),
                  pl.BlockSpec((CONV_HALO, ch), lambda i: (jnp.minimum((i + 1) * hb, n_halo_blocks - 1), 0)),
                  pl.BlockSpec((B_KERNEL, ch), lambda i: (0, 0)),
                  vec(), vec(), vec()],
        out_specs=pl.BlockSpec((tm, ch), lambda i: (i, 0)),
        scratch_shapes=[pltpu.VMEM((tm + 2 * CONV_HALO, ch), F32), pltpu.VMEM((tm, ch), F32)],
        compiler_params=_params(nbytes),
        name="conformer_conv",
    )(y, y, y, w_dw, b_dw.reshape(1, ch), ln_g.reshape(1, ch), ln_b.reshape(1, ch))


def _proj1_kernel(x_ref, mod_ref, g_ref, w1_ref, gq_ref, gkv_ref, wq_ref, wkv_ref, rope_ref,
                  q_ref, k_ref, v_ref, *, q_scale):
    mod = mod_ref[0]
    xm = (_rms(x_ref[...], g_ref[1:2]) * (1.0 + mod[4:5]) + mod[3:4]).astype(BF16)
    c1 = jnp.dot(xm, w1_ref[...], preferred_element_type=F32)
    cos, s_next, s_prev = rope_ref[:, 0:128], rope_ref[:, 128:256], rope_ref[:, 256:384]
    cq = _rms(c1[:, :C_Q_LORA], gq_ref[...] * q_scale).astype(BF16)
    ckv = _rms(c1[:, C_Q_LORA:C_Q_LORA + C_KV_LORA], gkv_ref[...]).astype(BF16)
    kr = _rope128(c1[:, C_Q_LORA + C_KV_LORA:], cos, s_next, s_prev).astype(BF16)
    q = jnp.dot(cq, wq_ref[...], preferred_element_type=F32)
    kv = jnp.dot(ckv, wkv_ref[...], preferred_element_type=F32)
    hp = C_HEAD_PAD
    for h in range(C_HEADS):
        q_ref[:, h * hp:h * hp + C_NOPE] = q[:, h * hp:h * hp + C_NOPE].astype(BF16)
        q_ref[:, h * hp + C_NOPE:(h + 1) * hp] = _rope128(
            q[:, h * hp + C_NOPE:(h + 1) * hp], cos, s_next, s_prev).astype(BF16)
        k_ref[:, h * hp:h * hp + C_NOPE] = kv[:, h * C_NOPE:(h + 1) * C_NOPE].astype(BF16)
        k_ref[:, h * hp + C_NOPE:(h + 1) * hp] = kr
    v_ref[...] = kv[:, C_HEADS * C_NOPE:].astype(BF16)


def _proj1(x, mod, g3, w1, gq, gkv, wq, wkv, rope_tab, *, tm, seq, n_lat_rows):
    rows, d = x.shape
    nb = mod.shape[0] - 1
    n_lat_tiles = n_lat_rows // tm
    tiles_per_seq = seq // tm
    rope_idx = lambda i: (jnp.where(i < n_lat_tiles, i % tiles_per_seq, tiles_per_seq), 0)
    qw = C_HEADS * C_HEAD_PAD
    vw = C_HEADS * C_V
    nbytes = (2 * (w1.size + wq.size + wkv.size) + 2 * tm * (d * 4 + 384 * 4 + (2 * qw + vw) * 2)
              + tm * (w1.shape[1] + wq.shape[1] + wkv.shape[1]) * 4 + tm * d * 6)
    row = lambda w: pl.BlockSpec((tm, w), lambda i: (i, 0))
    return pl.pallas_call(
        functools.partial(_proj1_kernel, q_scale=(C_NOPE + C_ROPE) ** -0.5),
        out_shape=(jax.ShapeDtypeStruct((rows, qw), BF16),
                   jax.ShapeDtypeStruct((rows, qw), BF16),
                   jax.ShapeDtypeStruct((rows, vw), BF16)),
        grid=(rows // tm,),
        in_specs=[row(d),
                  pl.BlockSpec((1, N_MOD, d), lambda i: (jnp.minimum(i * tm // seq, nb), 0, 0)),
                  _resident(g3.shape), _resident(w1.shape), _resident(gq.shape), _resident(gkv.shape),
                  _resident(wq.shape), _resident(wkv.shape),
                  pl.BlockSpec((tm, 384), rope_idx)],
        out_specs=(row(qw), row(qw), row(vw)),
        compiler_params=_params(nbytes),
        name="mla_proj",
    )(x, mod, g3, w1, gq, gkv, wq, wkv, rope_tab)


def _mla_kernel(q_ref, kl_ref, kc_ref, vl_ref, vc_ref, o_ref):
    hp = C_HEAD_PAD
    dn = (((1,), (1,)), ((), ()))
    for h in range(C_HEADS):
        qh = q_ref[:, h * hp:(h + 1) * hp]
        s1 = lax.dot_general(qh, kl_ref[:, h * hp:(h + 1) * hp], dn, preferred_element_type=F32)
        s2 = lax.dot_general(qh, kc_ref[:, h * hp:(h + 1) * hp], dn, preferred_element_type=F32)
        m = jnp.maximum(jnp.max(s1, axis=-1, keepdims=True), jnp.max(s2, axis=-1, keepdims=True))
        e1 = jnp.exp(s1 - m)
        e2 = jnp.exp(s2 - m)
        l = jnp.sum(e1, axis=-1, keepdims=True) + jnp.sum(e2, axis=-1, keepdims=True)
        o = (jnp.dot(e1.astype(BF16), vl_ref[:, h * C_V:(h + 1) * C_V], preferred_element_type=F32)
             + jnp.dot(e2.astype(BF16), vc_ref[:, h * C_V:(h + 1) * C_V], preferred_element_type=F32))
        o_ref[:, h * C_V:(h + 1) * C_V] = (o * (1.0 / l)).astype(BF16)


def _mla_attention(q, k, v, *, batch, seq, ctx, tq):
    n_lat = batch * seq
    qw, vw = k.shape[1], v.shape[1]
    nq = seq // tq
    ctx_blk0 = n_lat // ctx
    nbytes = (2 * (seq + ctx) * (qw + vw) * 2 + 2 * tq * (qw + vw) * 2 + 4 * tq * (seq + ctx) * 4)
    return pl.pallas_call(
        _mla_kernel,
        out_shape=jax.ShapeDtypeStruct((n_lat, vw), BF16),
        grid=(batch, nq),
        in_specs=[pl.BlockSpec((tq, qw), lambda b, i: (b * nq + i, 0)),
                  pl.BlockSpec((seq, qw), lambda b, i: (b, 0)),
                  pl.BlockSpec((ctx, qw), lambda b, i: (ctx_blk0 + b, 0)),
                  pl.BlockSpec((seq, vw), lambda b, i: (b, 0)),
                  pl.BlockSpec((ctx, vw), lambda b, i: (ctx_blk0 + b, 0))],
        out_specs=pl.BlockSpec((tq, vw), lambda b, i: (b * nq + i, 0)),
        compiler_params=_params(nbytes, 2),
        name="mla_attention",
    )(q, k, k, v, v)


def _pack_ffn(w_in, w_out, ck):
    d, f2 = w_in.shape
    f = f2 // 2
    n = f // ck
    g = w_in[:, :f].reshape(d, n, ck)
    u = w_in[:, f:].reshape(d, n, ck)
    win = jnp.concatenate([g, u], axis=-1).transpose(1, 0, 2)
    return win.astype(BF16), w_out.reshape(n, ck, w_out.shape[1]).astype(BF16)


def _rope_table(seq, pad_rows):
    t = jnp.arange(seq)
    row = (t // GRID_W).astype(F32)
    col = (t % GRID_W).astype(F32)
    d_axis = A_HEAD_DIM // 2
    inv_freq = ROPE_BASE ** (-jnp.arange(0, d_axis, 2, dtype=F32) / d_axis)
    ang_r = row[:, None] * inv_freq
    ang_c = col[:, None] * inv_freq
    ang = jnp.concatenate([ang_r, ang_r, ang_c, ang_c], axis=-1)
    cos, sin = jnp.cos(ang), jnp.sin(ang)
    first_half = (jnp.arange(A_HEAD_DIM) % 32) < 16
    s_next = jnp.where(first_half, -sin, 0.0)
    s_prev = jnp.where(first_half, 0.0, sin)
    tab = jnp.concatenate([jnp.tile(cos, (1, 2)), jnp.tile(s_next, (1, 2)), jnp.tile(s_prev, (1, 2))], axis=-1)
    ident = jnp.concatenate([jnp.ones((pad_rows, 128), F32), jnp.zeros((pad_rows, 256), F32)], axis=-1)
    return jnp.concatenate([tab, ident], axis=0)


def kernel(x, c, ctx, c_ctx, w_mod, b_mod, g_norm, ffn_w_in, ffn_w_out, ab_w_in, a_sink, b_w_dw, b_b_dw,
           b_ln_g, b_ln_b, ab_w_out, c_w_dq, c_g_q, c_w_uq, c_w_dkv, c_g_kv, c_w_uk, c_w_uv, c_w_o, g_final):
    batch, seq, d = x.shape
    ctx_len = ctx.shape[1]
    depth = w_mod.shape[0]
    d_ff = ffn_w_out.shape[2]
    n_lat = batch * seq
    rows = n_lat + batch * ctx_len
    tm = min(512, seq)
    tm_conv = min(256, ctx_len)
    tq = min(256, seq)
    ck = V7X_MXU_DIM
    assert seq % tm == 0 and (batch * ctx_len) % tm == 0 and seq % A_BLOCK == 0 and seq % GRID_W == 0
    assert ctx_len % tm_conv == 0 and seq % tm_conv == 0 and n_lat % ctx_len == 0 and d_ff % ck == 0
    assert depth % 2 == 0 and depth <= 2

    r_mod = -(-(batch + 1) // 16) * 16
    cc = jnp.concatenate([c, c_ctx[None, :], jnp.zeros((r_mod - batch - 1, d), F32)], axis=0)
    mod_all = _modulation(cc, w_mod, b_mod)[:, :batch + 1].reshape(depth, batch + 1, N_MOD, d)

    rope_tab = _rope_table(seq, tm)
    lane_masks = (jnp.arange(128)[None, :] // 64 == jnp.arange(2)[:, None]).astype(BF16)
    t = jnp.concatenate([x.reshape(n_lat, d), ctx.reshape(batch * ctx_len, d)], axis=0)

    for i in range(depth):
        last = i == depth - 1
        j = i // 2
        mod = mod_all[i]
        g3 = g_norm[i]
        win0, wout0 = _pack_ffn(ffn_w_in[i, 0], ffn_w_out[i, 0], ck)
        win1, wout1 = _pack_ffn(ffn_w_in[i, 1], ffn_w_out[i, 1], ck)
        t = _ffn(t, mod, g3, win0, wout0, rows=rows, tm=tm, seq=seq, sub=0)
        if i % 2 == 0:
            w = ab_w_in[j]
            iq, ik = A_HEADS * A_HEAD_DIM, A_KV_HEADS * A_HEAD_DIM
            wk, wv = w[:, iq:iq + ik], w[:, iq + ik:iq + 2 * ik]
            swap = lambda m: jnp.concatenate([m[:, A_HEAD_DIM:], m[:, :A_HEAD_DIM]], axis=1)
            w_all = jnp.concatenate([w[:, :iq], wk, swap(wk), wv, swap(wv), w[:, iq + 2 * ik:]], axis=1).astype(BF16)
            q, kv, y = _proj0(t, mod, g3, w_all, rope_tab, tm=tm, seq=seq, n_lat_rows=n_lat)
            a = _gqa(q, kv, a_sink[j].astype(F32), lane_masks, batch=batch, seq=seq, ctx=ctx_len)
            bx = _conformer_conv(y, b_w_dw[j], b_b_dw[j], b_ln_g[j], b_ln_b[j],
                                 tm=tm_conv, seq=seq, ctx=ctx_len, n_lat_rows=n_lat)
            wo = ab_w_out[j].astype(BF16)
            attn, wos = (a, bx), (wo[:iq], wo[iq:])
        else:
            hp = C_HEAD_PAD
            wdkv = c_w_dkv[j]
            w1 = jnp.concatenate([c_w_dq[j], wdkv, jnp.zeros((d, 128 - C_ROPE), F32)], axis=1).astype(BF16)
            wuq = c_w_uq[j].reshape(C_Q_LORA, C_HEADS, C_NOPE + C_ROPE)
            wuq = jnp.pad(wuq, ((0, 0), (0, 0), (0, hp - C_NOPE - C_ROPE))).reshape(C_Q_LORA, C_HEADS * hp)
            wkv = jnp.concatenate([c_w_uk[j], c_w_uv[j]], axis=1).astype(BF16)
            q, k, v = _proj1(t, mod, g3, w1, c_g_q[j].reshape(1, -1), c_g_kv[j].reshape(1, -1),
                             wuq.astype(BF16), wkv, rope_tab, tm=tm, seq=seq, n_lat_rows=n_lat)
            a = _mla_attention(q, k, v, batch=batch, seq=seq, ctx=ctx_len, tq=tq)
            attn, wos = (a,), (c_w_o[j].astype(BF16),)
        out_rows = n_lat if last else rows
        if last:
            attn = tuple(m[:n_lat] for m in attn)
        t = _ffn(t, mod, g3, win1, wout1, rows=out_rows, tm=tm, seq=seq, sub=2, attn=attn, wo=wos,
                 g_final=g_final if last else None)
    return t.reshape(batch, seq, d)
```

```python
import functools
import math

import jax
import jax.numpy as jnp
from jax import lax
from jax.experimental import pallas as pl
from jax.experimental.pallas import tpu as pltpu

F32 = jnp.float32
BF16 = jnp.bfloat16

GRID_W = 64
ROPE_BASE = 10000.0
NORM_EPS = 1e-6
LN_EPS = 1e-5
NEG_INF = -1e30
LOG2E = math.log2(math.e)
N_MOD = 9
FFN_RES = 0.5
A_HEADS = 8
A_KV_HEADS = 2
A_HEAD_DIM = 64
A_WINDOW = 128
A_BLOCK = 128
B_CH = 512
B_KERNEL = 31
C_HEADS = 8
C_Q_LORA = 256
C_KV_LORA = 256
C_NOPE = 128
C_ROPE = 64
C_V = 128

V7X_LANES = 128
V7X_MXU_DIM = 256
V7X_VMEM_BYTES = 64 * 1024 * 1024
V7X_VMEM_USABLE = 56 * 1024 * 1024

C_HEAD_PAD = V7X_MXU_DIM
CONV_SUBLANES = 8
CONV_HALO = 16
CONV_ROW_CHUNK = 64


def _vmem_limit(nbytes):
    return int(min(V7X_VMEM_USABLE, max(16 * 1024 * 1024, nbytes * 3 // 2)))


def _params(nbytes, ngrid=1):
    return pltpu.CompilerParams(dimension_semantics=("arbitrary",) * ngrid,
                                vmem_limit_bytes=_vmem_limit(nbytes))


def _rms(x, g):
    return x * lax.rsqrt(jnp.mean(x * x, axis=-1, keepdims=True) + NORM_EPS) * g


def _resident(shape):
    nd = len(shape)
    return pl.BlockSpec(shape, lambda *_: (0,) * nd, pipeline_mode=pl.Buffered(1))


def _mod_kernel(c_ref, w_ref, b_ref, o_ref):
    c = c_ref[...]
    a = (c * jax.nn.sigmoid(c)).astype(BF16)
    o_ref[0] = jnp.dot(a, w_ref[0].astype(BF16), preferred_element_type=F32) + b_ref[0]


def _modulation(cc, w_mod, b_mod):
    depth, d, n = w_mod.shape
    r = cc.shape[0]
    tn = n // N_MOD
    nbytes = 2 * (d * tn * 4) + d * tn * 2 + 4 * r * (d + 2 * tn) * 4
    return pl.pallas_call(
        _mod_kernel,
        out_shape=jax.ShapeDtypeStruct((depth, r, n), F32),
        grid=(depth, n // tn),
        in_specs=[pl.BlockSpec((r, d), lambda l, j: (0, 0)),
                  pl.BlockSpec((1, d, tn), lambda l, j: (l, 0, j)),
                  pl.BlockSpec((1, 1, tn), lambda l, j: (l, 0, j))],
        out_specs=pl.BlockSpec((1, r, tn), lambda l, j: (l, 0, j)),
        compiler_params=_params(nbytes, 2),
        name="adaln_modulation",
    )(cc, w_mod, b_mod.reshape(depth, 1, n))


def _row_specs(op, tm, n_lat_tiles):
    if not isinstance(op, tuple):
        return [pl.BlockSpec((tm, op.shape[1]), lambda i: (i, 0))]
    lat, cx = op
    return [pl.BlockSpec((tm, lat.shape[1]), lambda i: (jnp.minimum(i, n_lat_tiles - 1), 0)),
            pl.BlockSpec((tm, cx.shape[1]), lambda i: (jnp.maximum(i - n_lat_tiles, 0), 0))]


def _row_load(refs, is_lat):
    if len(refs) == 1:
        return refs[0][...]
    return jnp.where(is_lat, refs[0][...], refs[1][...])


def _ffn_kernel(*refs, arity, sub, final_norm, ck, n_lat_tiles):
    refs = list(refs)
    o_ref = refs.pop()
    take = lambda n: [refs.pop(0) for _ in range(n)]
    x_refs = take(arity[0])
    mod_ref, g_ref = take(2)
    a_refs = [take(n) for n in arity[1:]]
    wo_refs = take(len(arity) - 1)
    win_ref, wout_ref = take(2)
    gf_ref = refs.pop(0) if final_norm else None
    is_lat = pl.program_id(0) < n_lat_tiles

    x = _row_load(x_refs, is_lat)
    mod = mod_ref[0]
    if a_refs:
        ox = None
        for ar, wo_ref in zip(a_refs, wo_refs):
            part = jnp.dot(_row_load(ar, is_lat), wo_ref[...], preferred_element_type=F32)
            ox = part if ox is None else ox + part
        x = x + mod[5:6] * ox
    shift, scale, gate = mod[3 * sub:3 * sub + 1], mod[3 * sub + 1:3 * sub + 2], mod[3 * sub + 2:3 * sub + 3]
    h = (_rms(x, g_ref[sub:sub + 1]) * (1.0 + scale) + shift).astype(BF16)
    d_ff = wout_ref.shape[0]
    acc = None
    for c in range(d_ff // ck):
        g = jnp.dot(h, win_ref[:, c * ck:(c + 1) * ck], preferred_element_type=F32)
        u = jnp.dot(h, win_ref[:, d_ff + c * ck:d_ff + (c + 1) * ck], preferred_element_type=F32)
        act = (g * jax.nn.sigmoid(g) * u).astype(BF16)
        part = jnp.dot(act, wout_ref[c * ck:(c + 1) * ck, :], preferred_element_type=F32)
        acc = part if acc is None else acc + part
    y = x + (FFN_RES * gate) * acc
    if final_norm:
        y = _rms(y, gf_ref[...])
    o_ref[...] = y


def _ffn(x, mod, g3, win, wout, *, rows, tm, seq, n_lat_rows, sub, ck, attn=(), wo=(), g_final=None):
    d = win.shape[0]
    nb = mod.shape[0] - 1
    n_lat_tiles = n_lat_rows // tm
    arity = tuple(2 if isinstance(o, tuple) else 1 for o in (x, *attn))
    flat = lambda o: list(o) if isinstance(o, tuple) else [o]
    in_specs = _row_specs(x, tm, n_lat_tiles)
    in_specs += [pl.BlockSpec((1, N_MOD, d), lambda i: (jnp.minimum(i * tm // seq, nb), 0, 0)),
                 _resident(g3.shape)]
    for a in attn:
        in_specs += _row_specs(a, tm, n_lat_tiles)
    in_specs += [_resident(w.shape) for w in wo]
    in_specs += [_resident(win.shape), _resident(wout.shape)]
    args = [*flat(x), mod, g3, *[m for a in attn for m in flat(a)], *wo, win, wout]
    if g_final is not None:
        in_specs.append(_resident((1, d)))
        args.append(g_final.reshape(1, d))
    weights = 2 * (win.size + wout.size + sum(w.size for w in wo))
    tiles = 2 * tm * (arity[0] * d * 4 + d * 4 + sum(n * flat(a)[0].shape[1] for n, a in zip(arity[1:], attn)) * 2)
    temps = tm * (d * 4 * 3 + d * 2 + 2 * ck * 4 * 2 + ck * 2)
    kern = functools.partial(_ffn_kernel, arity=arity, sub=sub, final_norm=g_final is not None,
                             ck=ck, n_lat_tiles=n_lat_tiles)
    return pl.pallas_call(
        kern,
        out_shape=jax.ShapeDtypeStruct((rows, d), F32),
        grid=(rows // tm,),
        in_specs=in_specs,
        out_specs=pl.BlockSpec((tm, d), lambda i: (i, 0)),
        compiler_params=_params(weights + tiles + temps),
        name=f"ffn_sub{sub}" + ("_mix" if attn else "") + ("_final" if g_final is not None else ""),
    )(*args)


def _rope128(v, cos, sin_next, sin_prev):
    return (v * cos + pltpu.roll(v, V7X_LANES - 16, 1) * sin_next + pltpu.roll(v, 16, 1) * sin_prev)


def _proj0_kernel(x_ref, mod_ref, g_ref, w_ref, rope_ref, q_ref, kv_ref, y_ref, *, q_scale):
    mod = mod_ref[0]
    xm = (_rms(x_ref[...], g_ref[1:2]) * (1.0 + mod[4:5]) + mod[3:4]).astype(BF16)
    p = jnp.dot(xm, w_ref[...], preferred_element_type=F32)
    cos, s_next, s_prev = rope_ref[:, 0:128], rope_ref[:, 128:256], rope_ref[:, 256:384]
    nq = A_HEADS * A_HEAD_DIM // V7X_LANES
    for j in range(nq):
        blk = p[:, j * 128:(j + 1) * 128]
        q_ref[:, j * 128:(j + 1) * 128] = (_rope128(blk, cos, s_next, s_prev) * q_scale).astype(BF16)
    base = nq * 128
    for j in range(2):
        blk = p[:, base + j * 128:base + (j + 1) * 128]
        kv_ref[:, j * 128:(j + 1) * 128] = _rope128(blk, cos, s_next, s_prev).astype(BF16)
    kv_ref[:, 256:512] = p[:, base + 256:base + 512].astype(BF16)
    a = p[:, base + 512:base + 512 + B_CH]
    g = p[:, base + 512 + B_CH:base + 512 + 2 * B_CH]
    y_ref[...] = a * jax.nn.sigmoid(g)


def _proj0(x, mod, g3, w, rope_tab, *, tm, seq, n_lat_rows):
    rows, d = x.shape
    nb = mod.shape[0] - 1
    n_lat_tiles = n_lat_rows // tm
    tiles_per_seq = seq // tm
    rope_idx = lambda i: (jnp.where(i < n_lat_tiles, i % tiles_per_seq, tiles_per_seq), 0)
    nw = w.shape[1]
    nbytes = 2 * w.size + 2 * tm * (d * 4 + 384 * 4 + 1024 * 2 + 512 * 4) + tm * (nw * 4 + d * 6)
    return pl.pallas_call(
        functools.partial(_proj0_kernel, q_scale=A_HEAD_DIM ** -0.5 * LOG2E),
        out_shape=(jax.ShapeDtypeStruct((rows, 512), BF16),
                   jax.ShapeDtypeStruct((rows, 512), BF16),
                   jax.ShapeDtypeStruct((rows, B_CH), F32)),
        grid=(rows // tm,),
        in_specs=[pl.BlockSpec((tm, d), lambda i: (i, 0)),
                  pl.BlockSpec((1, N_MOD, d), lambda i: (jnp.minimum(i * tm // seq, nb), 0, 0)),
                  _resident(g3.shape), _resident(w.shape),
                  pl.BlockSpec((tm, 384), rope_idx)],
        out_specs=(pl.BlockSpec((tm, 512), lambda i: (i, 0)),
                   pl.BlockSpec((tm, 512), lambda i: (i, 0)),
                   pl.BlockSpec((tm, B_CH), lambda i: (i, 0))),
        compiler_params=_params(nbytes),
        name="ab_in_proj",
    )(x, mod, g3, w, rope_tab)


def _gqa_core(q_ref, kv_blocks, lane_ref, sink_ref, o_ref, mask):
    r = q_ref.shape[0]
    m_lo, m_hi = lane_ref[0:1, :], lane_ref[1:2, :]
    kv = jnp.concatenate(kv_blocks, axis=0) if len(kv_blocks) > 1 else kv_blocks[0]
    nk = kv.shape[0]
    k, k_sw, v, v_sw = kv[:, 0:128], kv[:, 128:256], kv[:, 256:384], kv[:, 384:512]
    f_lo, f_hi = m_lo.astype(F32), m_hi.astype(F32)
    for hk in range(A_KV_HEADS):
        k_lo, k_hi = (k * m_lo, k_sw * m_hi) if hk == 0 else (k_sw * m_lo, k * m_hi)
        v_lo, v_hi = (v * m_lo, v_sw * m_hi) if hk == 0 else (v_sw * m_lo, v * m_hi)
        qs = jnp.concatenate([q_ref[:, (2 * hk) * 128:(2 * hk + 1) * 128],
                              q_ref[:, (2 * hk + 1) * 128:(2 * hk + 2) * 128]], axis=0)
        kcat = jnp.concatenate([k_lo, k_hi], axis=0)
        vcat = jnp.concatenate([v_lo, v_hi], axis=0)
        s = lax.dot_general(qs, kcat, (((1,), (1,)), ((), ())), preferred_element_type=F32)
        ps, rinv = [], []
        for j in range(2):
            pj, rj = [], []
            for par in range(2):
                sj = s[j * r:(j + 1) * r, par * nk:(par + 1) * nk]
                if mask is not None:
                    sj = jnp.where(mask, sj, NEG_INF)
                sink = sink_ref[4 * hk + 2 * j + par] * LOG2E
                m = jnp.maximum(jnp.max(sj, axis=-1, keepdims=True), sink)
                e = jnp.exp2(sj - m)
                l = jnp.sum(e, axis=-1, keepdims=True) + jnp.exp2(sink - m)
                pj.append(e.astype(BF16))
                rj.append(1.0 / l)
            ps.append(jnp.concatenate(pj, axis=1))
            rinv.append(rj[0] * f_lo + rj[1] * f_hi)
        p = jnp.concatenate(ps, axis=0)
        o = jnp.dot(p, vcat, preferred_element_type=F32)
        for j in range(2):
            blk = 2 * hk + j
            o_ref[:, blk * 128:(blk + 1) * 128] = (o[j * r:(j + 1) * r] * rinv[j]).astype(BF16)


def _gqa_window_kernel(sink_ref, q_ref, kvp_ref, kvc_ref, kvn_ref, kvx_ref, lane_ref, o_ref, *, nblk):
    n = pl.program_id(1)
    r = q_ref.shape[0]
    ctx = kvx_ref.shape[0]
    nk = 3 * A_BLOCK + ctx
    i = lax.broadcasted_iota(jnp.int32, (r, nk), 0)
    j = lax.broadcasted_iota(jnp.int32, (r, nk), 1)
    lo = jnp.where(n > 0, 0, A_BLOCK)
    hi = jnp.where(n < nblk - 1, 3 * A_BLOCK, 2 * A_BLOCK)
    band = (j >= i) & (j <= i + 2 * A_WINDOW) & (j >= lo) & (j < hi)
    mask = band | (j >= 3 * A_BLOCK)
    _gqa_core(q_ref, [kvp_ref[...], kvc_ref[...], kvn_ref[...], kvx_ref[...]], lane_ref, sink_ref, o_ref, mask)


def _gqa_ctx_kernel(sink_ref, q_ref, kvx_ref, lane_ref, o_ref):
    _gqa_core(q_ref, [kvx_ref[...]], lane_ref, sink_ref, o_ref, None)


def _gqa(q, kv, sink, lane_masks, *, batch, seq, ctx):
    rows = q.shape[0]
    nblk = seq // A_BLOCK
    n_lat = batch * seq
    ctx_blk0 = n_lat // ctx
    smem = pl.BlockSpec(memory_space=pltpu.SMEM)
    lane_spec = lambda nd: pl.BlockSpec((2, 128), lambda *_: (0, 0))
    row_blk = lambda f: pl.BlockSpec((A_BLOCK, 512), f)
    nbytes = 2 * (5 * A_BLOCK + ctx) * 512 * 2 + 6 * (2 * A_BLOCK) * 2 * (3 * A_BLOCK + ctx) * 4
    o_lat = pl.pallas_call(
        functools.partial(_gqa_window_kernel, nblk=nblk),
        out_shape=jax.ShapeDtypeStruct((n_lat, 512), BF16),
        grid=(batch, nblk),
        in_specs=[smem,
                  row_blk(lambda b, n: (b * nblk + n, 0)),
                  row_blk(lambda b, n: (b * nblk + jnp.maximum(n - 1, 0), 0)),
                  row_blk(lambda b, n: (b * nblk + n, 0)),
                  row_blk(lambda b, n: (b * nblk + jnp.minimum(n + 1, nblk - 1), 0)),
                  pl.BlockSpec((ctx, 512), lambda b, n: (ctx_blk0 + b, 0)),
                  lane_spec(2)],
        out_specs=row_blk(lambda b, n: (b * nblk + n, 0)),
        compiler_params=_params(nbytes, 2),
        name="gqa_window",
    )(sink, q, kv, kv, kv, kv, lane_masks)
    nbytes_c = 2 * 3 * ctx * 512 * 2 + 6 * (2 * ctx) * (2 * ctx) * 4
    o_ctx = pl.pallas_call(
        _gqa_ctx_kernel,
        out_shape=jax.ShapeDtypeStruct((rows - n_lat, 512), BF16),
        grid=(batch,),
        in_specs=[smem,
                  pl.BlockSpec((ctx, 512), lambda b: (ctx_blk0 + b, 0)),
                  pl.BlockSpec((ctx, 512), lambda b: (ctx_blk0 + b, 0)),
                  lane_spec(1)],
        out_specs=pl.BlockSpec((ctx, 512), lambda b: (b, 0)),
        compiler_params=_params(nbytes_c, 1),
        name="gqa_context",
    )(sink, q, kv, lane_masks)
    return o_lat, o_ctx


def _conv_kernel(yp_ref, yc_ref, yn_ref, w_ref, b_ref, g_ref, beta_ref, o_ref, ext_ref, acc_ref, *,
                 n_lat_tiles, lat_tiles_per_seq, ctx_tiles_per_seq):
    i = pl.program_id(0)
    tm, ch = yc_ref.shape
    is_lat = i < n_lat_tiles
    pos = jnp.where(is_lat, i % lat_tiles_per_seq, (i - n_lat_tiles) % ctx_tiles_per_seq)
    last_pos = jnp.where(is_lat, lat_tiles_per_seq - 1, ctx_tiles_per_seq - 1)
    keep_prev = (pos > 0).astype(F32)
    keep_next = (pos < last_pos).astype(F32)
    ext_ref[0:CONV_HALO, :] = yp_ref[...] * keep_prev
    ext_ref[CONV_HALO:CONV_HALO + tm, :] = yc_ref[...]
    ext_ref[CONV_HALO + tm:, :] = yn_ref[...] * keep_next
    off = CONV_HALO - B_KERNEL // 2
    sub = CONV_SUBLANES
    n_a = -(-(B_KERNEL + off) // sub)
    rc = min(CONV_ROW_CHUNK, tm)
    for cb in range(ch // V7X_LANES):
        cs = slice(cb * V7X_LANES, (cb + 1) * V7X_LANES)
        for r0 in range(0, tm, rc):
            acc = None
            for s in range(sub):
                z = None
                for a in range(n_a):
                    j = sub * a + s - off
                    if 0 <= j < B_KERNEL:
                        term = w_ref[j:j + 1, cs] * ext_ref[r0 + sub * a:r0 + sub * a + rc + sub, cs]
                        z = term if z is None else z + term
                zs = z[s:s + rc, :]
                acc = zs if acc is None else acc + zs
            acc_ref[r0:r0 + rc, cs] = acc + b_ref[:, cs]
    y = acc_ref[...]
    mu = jnp.mean(y, axis=-1, keepdims=True)
    yc = y - mu
    var = jnp.mean(yc * yc, axis=-1, keepdims=True)
    z = yc * lax.rsqrt(var + LN_EPS) * g_ref[...] + beta_ref[...]
    o_ref[...] = (z * jax.nn.sigmoid(z)).astype(BF16)


def _conformer_conv(y, w_dw, b_dw, ln_g, ln_b, *, tm, seq, ctx, n_lat_rows):
    rows, ch = y.shape
    hb = tm // CONV_HALO
    n_halo_blocks = rows // CONV_HALO
    kern = functools.partial(_conv_kernel, n_lat_tiles=n_lat_rows // tm,
                             lat_tiles_per_seq=seq // tm, ctx_tiles_per_seq=ctx // tm)
    nbytes = 2 * (tm + 2 * CONV_HALO) * ch * 4 + 2 * tm * ch * 2 + (2 * tm + 2 * CONV_HALO) * ch * 4 + 6 * tm * ch * 4
    vec = lambda: pl.BlockSpec((1, ch), lambda i: (0, 0))
    return pl.pallas_call(
        kern,
        out_shape=jax.ShapeDtypeStruct((rows, ch), BF16),
        grid=(rows // tm,),
        in_specs=[pl.BlockSpec((CONV_HALO, ch), lambda i: (jnp.maximum(i * hb - 1, 0), 0)),
                  pl.BlockSpec((tm, ch), lambda i: (i, 0)),
                  pl.BlockSpec((CONV_HALO, ch), lambda i: (jnp.minimum((i + 1) * hb, n_halo_blocks - 1), 0)),
                  pl.BlockSpec((B_KERNEL, ch), lambda i: (0, 0)),
                  vec(), vec(), vec()],
        out_specs=pl.BlockSpec((tm, ch), lambda i: (i, 0)),
        scratch_shapes=[pltpu.VMEM((tm + 2 * CONV_HALO, ch), F32), pltpu.VMEM((tm, ch), F32)],
        compiler_params=_params(nbytes),
        name="conformer_conv",
    )(y, y, y, w_dw, b_dw.reshape(1, ch), ln_g.reshape(1, ch), ln_b.reshape(1, ch))


def _proj1_kernel(x_ref, mod_ref, g_ref, w1_ref, gq_ref, gkv_ref, wq_ref, wkv_ref, rope_ref,
                  q_ref, k_ref, v_ref, *, q_scale):
    mod = mod_ref[0]
    xm = (_rms(x_ref[...], g_ref[1:2]) * (1.0 + mod[4:5]) + mod[3:4]).astype(BF16)
    c1 = jnp.dot(xm, w1_ref[...], preferred_element_type=F32)
    cos, s_next, s_prev = rope_ref[:, 0:128], rope_ref[:, 128:256], rope_ref[:, 256:384]
    cq = _rms(c1[:, :C_Q_LORA], gq_ref[...] * q_scale).astype(BF16)
    ckv = _rms(c1[:, C_Q_LORA:C_Q_LORA + C_KV_LORA], gkv_ref[...]).astype(BF16)
    kr = _rope128(c1[:, C_Q_LORA + C_KV_LORA:], cos, s_next, s_prev).astype(BF16)
    q = jnp.dot(cq, wq_ref[...], preferred_element_type=F32)
    kv = jnp.dot(ckv, wkv_ref[...], preferred_element_type=F32)
    hp = C_HEAD_PAD
    for h in range(C_HEADS):
        q_ref[:, h * hp:h * hp + C_NOPE] = q[:, h * hp:h * hp + C_NOPE].astype(BF16)
        q_ref[:, h * hp + C_NOPE:(h + 1) * hp] = _rope128(
            q[:, h * hp + C_NOPE:(h + 1) * hp], cos, s_next, s_prev).astype(BF16)
        k_ref[:, h * hp:h * hp + C_NOPE] = kv[:, h * C_NOPE:(h + 1) * C_NOPE].astype(BF16)
        k_ref[:, h * hp + C_NOPE:(h + 1) * hp] = kr
    v_ref[...] = kv[:, C_HEADS * C_NOPE:].astype(BF16)


def _proj1(x, mod, g3, w1, gq, gkv, wq, wkv, rope_tab, *, tm, seq, n_lat_rows):
    rows, d = x.shape
    nb = mod.shape[0] - 1
    n_lat_tiles = n_lat_rows // tm
    tiles_per_seq = seq // tm
    rope_idx = lambda i: (jnp.where(i < n_lat_tiles, i % tiles_per_seq, tiles_per_seq), 0)
    qw = C_HEADS * C_HEAD_PAD
    vw = C_HEADS * C_V
    nbytes = (2 * (w1.size + wq.size + wkv.size) + 2 * tm * (d * 4 + 384 * 4 + (2 * qw + vw) * 2)
              + tm * (w1.shape[1] + wq.shape[1] + wkv.shape[1]) * 4 + tm * d * 6)
    row = lambda w: pl.BlockSpec((tm, w), lambda i: (i, 0))
    return pl.pallas_call(
        functools.partial(_proj1_kernel, q_scale=(C_NOPE + C_ROPE) ** -0.5 * LOG2E),
        out_shape=(jax.ShapeDtypeStruct((rows, qw), BF16),
                   jax.ShapeDtypeStruct((rows, qw), BF16),
                   jax.ShapeDtypeStruct((rows, vw), BF16)),
        grid=(rows // tm,),
        in_specs=[row(d),
                  pl.BlockSpec((1, N_MOD, d), lambda i: (jnp.minimum(i * tm // seq, nb), 0, 0)),
                  _resident(g3.shape), _resident(w1.shape), _resident(gq.shape), _resident(gkv.shape),
                  _resident(wq.shape), _resident(wkv.shape),
                  pl.BlockSpec((tm, 384), rope_idx)],
        out_specs=(row(qw), row(qw), row(vw)),
        compiler_params=_params(nbytes),
        name="mla_proj",
    )(x, mod, g3, w1, gq, gkv, wq, wkv, rope_tab)


def _mla_kernel(q_ref, kl_ref, kc_ref, vl_ref, vc_ref, o_ref):
    hp = C_HEAD_PAD
    dn = (((1,), (1,)), ((), ()))
    for h in range(C_HEADS):
        qh = q_ref[:, h * hp:(h + 1) * hp]
        s1 = lax.dot_general(qh, kl_ref[:, h * hp:(h + 1) * hp], dn, preferred_element_type=F32)
        s2 = lax.dot_general(qh, kc_ref[:, h * hp:(h + 1) * hp], dn, preferred_element_type=F32)
        m = jnp.maximum(jnp.max(s1, axis=-1, keepdims=True), jnp.max(s2, axis=-1, keepdims=True))
        e1 = jnp.exp2(s1 - m)
        e2 = jnp.exp2(s2 - m)
        l = jnp.sum(e1, axis=-1, keepdims=True) + jnp.sum(e2, axis=-1, keepdims=True)
        o = (jnp.dot(e1.astype(BF16), vl_ref[:, h * C_V:(h + 1) * C_V], preferred_element_type=F32)
             + jnp.dot(e2.astype(BF16), vc_ref[:, h * C_V:(h + 1) * C_V], preferred_element_type=F32))
        o_ref[:, h * C_V:(h + 1) * C_V] = (o * (1.0 / l)).astype(BF16)


def _mla_attention(q, k, v, *, batch, seq, ctx, tq):
    n_lat = batch * seq
    qw, vw = k.shape[1], v.shape[1]
    nq = seq // tq
    ctx_blk0 = n_lat // ctx
    nbytes = (2 * (seq + ctx) * (qw + vw) * 2 + 2 * tq * (qw + vw) * 2 + 4 * tq * (seq + ctx) * 4)
    return pl.pallas_call(
        _mla_kernel,
        out_shape=jax.ShapeDtypeStruct((n_lat, vw), BF16),
        grid=(batch, nq),
        in_specs=[pl.BlockSpec((tq, qw), lambda b, i: (b * nq + i, 0)),
                  pl.BlockSpec((seq, qw), lambda b, i: (b, 0)),
                  pl.BlockSpec((ctx, qw), lambda b, i: (ctx_blk0 + b, 0)),
                  pl.BlockSpec((seq, vw), lambda b, i: (b, 0)),
                  pl.BlockSpec((ctx, vw), lambda b, i: (ctx_blk0 + b, 0))],
        out_specs=pl.BlockSpec((tq, vw), lambda b, i: (b * nq + i, 0)),
        compiler_params=_params(nbytes, 2),
        name="mla_attention",
    )(q, k, k, v, v)


def _rope_table(seq, pad_rows):
    t = jnp.arange(seq)
    row = (t // GRID_W).astype(F32)
    col = (t % GRID_W).astype(F32)
    d_axis = A_HEAD_DIM // 2
    inv_freq = ROPE_BASE ** (-jnp.arange(0, d_axis, 2, dtype=F32) / d_axis)
    ang_r = row[:, None] * inv_freq
    ang_c = col[:, None] * inv_freq
    ang = jnp.concatenate([ang_r, ang_r, ang_c, ang_c], axis=-1)
    cos, sin = jnp.cos(ang), jnp.sin(ang)
    first_half = (jnp.arange(A_HEAD_DIM) % 32) < 16
    s_next = jnp.where(first_half, -sin, 0.0)
    s_prev = jnp.where(first_half, 0.0, sin)
    tab = jnp.concatenate([jnp.tile(cos, (1, 2)), jnp.tile(s_next, (1, 2)), jnp.tile(s_prev, (1, 2))], axis=-1)
    ident = jnp.concatenate([jnp.ones((pad_rows, 128), F32), jnp.zeros((pad_rows, 256), F32)], axis=-1)
    return jnp.concatenate([tab, ident], axis=0)


def kernel(x, c, ctx, c_ctx, w_mod, b_mod, g_norm, ffn_w_in, ffn_w_out, ab_w_in, a_sink, b_w_dw, b_b_dw,
           b_ln_g, b_ln_b, ab_w_out, c_w_dq, c_g_q, c_w_uq, c_w_dkv, c_g_kv, c_w_uk, c_w_uv, c_w_o, g_final):
    batch, seq, d = x.shape
    ctx_len = ctx.shape[1]
    depth = w_mod.shape[0]
    d_ff = ffn_w_out.shape[2]
    n_lat = batch * seq
    rows = n_lat + batch * ctx_len
    tm = min(512, seq)
    tm_conv = min(256, ctx_len)
    tq = min(256, seq)
    ck = V7X_MXU_DIM
    assert seq % tm == 0 and (batch * ctx_len) % tm == 0 and seq % A_BLOCK == 0 and seq % GRID_W == 0
    assert ctx_len % tm_conv == 0 and seq % tm_conv == 0 and n_lat % ctx_len == 0 and d_ff % ck == 0
    assert depth % 2 == 0 and depth <= 2

    r_mod = -(-(batch + 1) // 16) * 16
    cc = jnp.concatenate([c, c_ctx[None, :], jnp.zeros((r_mod - batch - 1, d), F32)], axis=0)
    mod_all = _modulation(cc, w_mod, b_mod)[:, :batch + 1].reshape(depth, batch + 1, N_MOD, d)

    rope_tab = _rope_table(seq, tm)
    lane_masks = (jnp.arange(128)[None, :] // 64 == jnp.arange(2)[:, None]).astype(BF16)
    t = (x.reshape(n_lat, d), ctx.reshape(batch * ctx_len, d))
    ffn_win, ffn_wout = ffn_w_in.astype(BF16), ffn_w_out.astype(BF16)

    for i in range(depth):
        last = i == depth - 1
        j = i // 2
        mod = mod_all[i]
        g3 = g_norm[i]
        t = _ffn(t, mod, g3, ffn_win[i, 0], ffn_wout[i, 0], rows=rows, tm=tm, seq=seq, n_lat_rows=n_lat,
                 sub=0, ck=ck)
        if i % 2 == 0:
            w = ab_w_in[j]
            iq, ik = A_HEADS * A_HEAD_DIM, A_KV_HEADS * A_HEAD_DIM
            wk, wv = w[:, iq:iq + ik], w[:, iq + ik:iq + 2 * ik]
            swap = lambda m: jnp.concatenate([m[:, A_HEAD_DIM:], m[:, :A_HEAD_DIM]], axis=1)
            w_all = jnp.concatenate([w[:, :iq], wk, swap(wk), wv, swap(wv), w[:, iq + 2 * ik:]], axis=1).astype(BF16)
            q, kv, y = _proj0(t, mod, g3, w_all, rope_tab, tm=tm, seq=seq, n_lat_rows=n_lat)
            a = _gqa(q, kv, a_sink[j].astype(F32), lane_masks, batch=batch, seq=seq, ctx=ctx_len)
            bx = _conformer_conv(y, b_w_dw[j], b_b_dw[j], b_ln_g[j], b_ln_b[j],
                                 tm=tm_conv, seq=seq, ctx=ctx_len, n_lat_rows=n_lat)
            wo = ab_w_out[j].astype(BF16)
            attn, wos = (a, bx), (wo[:iq], wo[iq:])
        else:
            hp = C_HEAD_PAD
            wdkv = c_w_dkv[j]
            w1 = jnp.concatenate([c_w_dq[j], wdkv, jnp.zeros((d, 128 - C_ROPE), F32)], axis=1).astype(BF16)
            wuq = c_w_uq[j].reshape(C_Q_LORA, C_HEADS, C_NOPE + C_ROPE)
            wuq = jnp.pad(wuq, ((0, 0), (0, 0), (0, hp - C_NOPE - C_ROPE))).reshape(C_Q_LORA, C_HEADS * hp)
            wkv = jnp.concatenate([c_w_uk[j], c_w_uv[j]], axis=1).astype(BF16)
            q, k, v = _proj1(t, mod, g3, w1, c_g_q[j].reshape(1, -1), c_g_kv[j].reshape(1, -1),
                             wuq.astype(BF16), wkv, rope_tab, tm=tm, seq=seq, n_lat_rows=n_lat)
            a = _mla_attention(q, k, v, batch=batch, seq=seq, ctx=ctx_len, tq=tq)
            attn, wos = (a,), (c_w_o[j].astype(BF16),)
        t = _ffn(t, mod, g3, ffn_win[i, 1], ffn_wout[i, 1], rows=n_lat if last else rows, tm=tm, seq=seq,
                 n_lat_rows=n_lat, sub=2, ck=ck, attn=attn, wo=wos, g_final=g_final if last else None)
    return t.reshape(batch, seq, d)
```

```python
import functools
import math

import jax
import jax.numpy as jnp
from jax import lax
from jax.experimental import pallas as pl
from jax.experimental.pallas import tpu as pltpu

F32 = jnp.float32
BF16 = jnp.bfloat16

GRID_W = 64
ROPE_BASE = 10000.0
NORM_EPS = 1e-6
LN_EPS = 1e-5
NEG_INF = -1e30
LOG2E = math.log2(math.e)
N_MOD = 9
FFN_RES = 0.5
A_HEADS = 8
A_KV_HEADS = 2
A_HEAD_DIM = 64
A_WINDOW = 128
A_BLOCK = 128
B_CH = 512
B_KERNEL = 31
C_HEADS = 8
C_Q_LORA = 256
C_KV_LORA = 256
C_NOPE = 128
C_ROPE = 64
C_V = 128

V7X_LANES = 128
V7X_MXU_DIM = 256
V7X_VMEM_BYTES = 64 * 1024 * 1024
V7X_VMEM_USABLE = 56 * 1024 * 1024

C_HEAD_PAD = V7X_MXU_DIM
V7X_SUBLANES = 8
CONV_HALO = 16
CONV_ROW_CHUNK = 64


def _vmem_limit(nbytes):
    return int(min(V7X_VMEM_USABLE, max(16 * 1024 * 1024, nbytes * 3 // 2)))


def _params(nbytes, ngrid=1):
    return pltpu.CompilerParams(dimension_semantics=("arbitrary",) * ngrid,
                                vmem_limit_bytes=_vmem_limit(nbytes))


def _rms(x, g):
    return x * lax.rsqrt(jnp.mean(x * x, axis=-1, keepdims=True) + NORM_EPS) * g


def _resident(shape):
    nd = len(shape)
    return pl.BlockSpec(shape, lambda *_: (0,) * nd, pipeline_mode=pl.Buffered(1))


def _mod_kernel(c_ref, w_ref, b_ref, o_ref):
    c = c_ref[...]
    a = (c * jax.nn.sigmoid(c)).astype(BF16)
    o_ref[0] = jnp.dot(a, w_ref[0].astype(BF16), preferred_element_type=F32) + b_ref[0]


def _modulation(cc, w_mod, b_mod):
    depth, d, n = w_mod.shape
    r = cc.shape[0]
    tn = n // N_MOD
    nbytes = 2 * (d * tn * 4) + d * tn * 2 + 4 * r * (d + 2 * tn) * 4
    return pl.pallas_call(
        _mod_kernel,
        out_shape=jax.ShapeDtypeStruct((depth, r, n), F32),
        grid=(depth, n // tn),
        in_specs=[pl.BlockSpec((r, d), lambda l, j: (0, 0)),
                  pl.BlockSpec((1, d, tn), lambda l, j: (l, 0, j)),
                  pl.BlockSpec((1, 1, tn), lambda l, j: (l, 0, j))],
        out_specs=pl.BlockSpec((1, r, tn), lambda l, j: (l, 0, j)),
        compiler_params=_params(nbytes, 2),
        name="adaln_modulation",
    )(cc, w_mod, b_mod.reshape(depth, 1, n))


def _row_specs(op, tm, n_lat_tiles):
    if not isinstance(op, tuple):
        return [pl.BlockSpec((tm, op.shape[1]), lambda i: (i, 0))]
    lat, cx = op
    return [pl.BlockSpec((tm, lat.shape[1]), lambda i: (jnp.minimum(i, n_lat_tiles - 1), 0)),
            pl.BlockSpec((tm, cx.shape[1]), lambda i: (jnp.maximum(i - n_lat_tiles, 0), 0))]


def _row_load(refs, is_lat):
    if len(refs) == 1:
        return refs[0][...]
    return jnp.where(is_lat, refs[0][...], refs[1][...])


def _ffn_kernel(*refs, arity, sub, final_norm, ck, n_lat_tiles):
    refs = list(refs)
    o_ref = refs.pop()
    take = lambda n: [refs.pop(0) for _ in range(n)]
    x_refs = take(arity[0])
    mod_ref, g_ref = take(2)
    a_refs = [take(n) for n in arity[1:]]
    wo_refs = take(len(arity) - 1)
    win_ref, wout_ref = take(2)
    gf_ref = refs.pop(0) if final_norm else None
    is_lat = pl.program_id(0) < n_lat_tiles

    x = _row_load(x_refs, is_lat)
    mod = mod_ref[0]
    if a_refs:
        ox = None
        for ar, wo_ref in zip(a_refs, wo_refs):
            part = jnp.dot(_row_load(ar, is_lat), wo_ref[...], preferred_element_type=F32)
            ox = part if ox is None else ox + part
        x = x + mod[5:6] * ox
    shift, scale, gate = mod[3 * sub:3 * sub + 1], mod[3 * sub + 1:3 * sub + 2], mod[3 * sub + 2:3 * sub + 3]
    h = (_rms(x, g_ref[sub:sub + 1]) * (1.0 + scale) + shift).astype(BF16)
    d_ff = wout_ref.shape[0]
    acc = None
    for c in range(d_ff // ck):
        g = jnp.dot(h, win_ref[:, c * ck:(c + 1) * ck], preferred_element_type=F32)
        u = jnp.dot(h, win_ref[:, d_ff + c * ck:d_ff + (c + 1) * ck], preferred_element_type=F32)
        act = (g * jax.nn.sigmoid(g) * u).astype(BF16)
        part = jnp.dot(act, wout_ref[c * ck:(c + 1) * ck, :], preferred_element_type=F32)
        acc = part if acc is None else acc + part
    y = x + (FFN_RES * gate) * acc
    if final_norm:
        y = _rms(y, gf_ref[...])
    o_ref[...] = y


def _ffn(x, mod, g3, win, wout, *, rows, tm, seq, n_lat_rows, sub, ck, attn=(), wo=(), g_final=None):
    d = win.shape[0]
    nb = mod.shape[0] - 1
    n_lat_tiles = n_lat_rows // tm
    arity = tuple(2 if isinstance(o, tuple) else 1 for o in (x, *attn))
    flat = lambda o: list(o) if isinstance(o, tuple) else [o]
    in_specs = _row_specs(x, tm, n_lat_tiles)
    in_specs += [pl.BlockSpec((1, N_MOD, d), lambda i: (jnp.minimum(i * tm // seq, nb), 0, 0)),
                 _resident(g3.shape)]
    for a in attn:
        in_specs += _row_specs(a, tm, n_lat_tiles)
    in_specs += [_resident(w.shape) for w in wo]
    in_specs += [_resident(win.shape), _resident(wout.shape)]
    args = [*flat(x), mod, g3, *[m for a in attn for m in flat(a)], *wo, win, wout]
    if g_final is not None:
        in_specs.append(_resident((1, d)))
        args.append(g_final.reshape(1, d))
    weights = 2 * (win.size + wout.size + sum(w.size for w in wo))
    tiles = 2 * tm * (arity[0] * d * 4 + d * 4 + sum(n * flat(a)[0].shape[1] for n, a in zip(arity[1:], attn)) * 2)
    temps = tm * (d * 4 * 3 + d * 2 + 2 * ck * 4 * 2 + ck * 2)
    kern = functools.partial(_ffn_kernel, arity=arity, sub=sub, final_norm=g_final is not None,
                             ck=ck, n_lat_tiles=n_lat_tiles)
    return pl.pallas_call(
        kern,
        out_shape=jax.ShapeDtypeStruct((rows, d), F32),
        grid=(rows // tm,),
        in_specs=in_specs,
        out_specs=pl.BlockSpec((tm, d), lambda i: (i, 0)),
        compiler_params=_params(weights + tiles + temps),
        name=f"ffn_sub{sub}" + ("_mix" if attn else "") + ("_final" if g_final is not None else ""),
    )(*args)


def _rope128(v, cos, sin_next, sin_prev):
    return (v * cos + pltpu.roll(v, V7X_LANES - 16, 1) * sin_next + pltpu.roll(v, 16, 1) * sin_prev)


def _proj0_kernel(x_ref, mod_ref, g_ref, w_ref, rope_ref, q_ref, kv_ref, y_ref, *, q_scale):
    mod = mod_ref[0]
    xm = (_rms(x_ref[...], g_ref[1:2]) * (1.0 + mod[4:5]) + mod[3:4]).astype(BF16)
    p = jnp.dot(xm, w_ref[...], preferred_element_type=F32)
    cos, s_next, s_prev = rope_ref[:, 0:128], rope_ref[:, 128:256], rope_ref[:, 256:384]
    nq = A_HEADS * A_HEAD_DIM // V7X_LANES
    for j in range(nq):
        blk = p[:, j * 128:(j + 1) * 128]
        q_ref[:, j * 128:(j + 1) * 128] = (_rope128(blk, cos, s_next, s_prev) * q_scale).astype(BF16)
    base = nq * 128
    for j in range(2):
        blk = p[:, base + j * 128:base + (j + 1) * 128]
        kv_ref[:, j * 128:(j + 1) * 128] = _rope128(blk, cos, s_next, s_prev).astype(BF16)
    kv_ref[:, 256:512] = p[:, base + 256:base + 512].astype(BF16)
    a = p[:, base + 512:base + 512 + B_CH]
    g = p[:, base + 512 + B_CH:base + 512 + 2 * B_CH]
    y_ref[...] = a * jax.nn.sigmoid(g)


def _proj0(x, mod, g3, w, rope_tab, *, tm, seq, n_lat_rows):
    rows, d = x.shape
    nb = mod.shape[0] - 1
    n_lat_tiles = n_lat_rows // tm
    tiles_per_seq = seq // tm
    rope_idx = lambda i: (jnp.where(i < n_lat_tiles, i % tiles_per_seq, tiles_per_seq), 0)
    nw = w.shape[1]
    nbytes = 2 * w.size + 2 * tm * (d * 4 + 384 * 4 + 1024 * 2 + 512 * 4) + tm * (nw * 4 + d * 6)
    return pl.pallas_call(
        functools.partial(_proj0_kernel, q_scale=A_HEAD_DIM ** -0.5 * LOG2E),
        out_shape=(jax.ShapeDtypeStruct((rows, 512), BF16),
                   jax.ShapeDtypeStruct((rows, 512), BF16),
                   jax.ShapeDtypeStruct((rows, B_CH), F32)),
        grid=(rows // tm,),
        in_specs=[pl.BlockSpec((tm, d), lambda i: (i, 0)),
                  pl.BlockSpec((1, N_MOD, d), lambda i: (jnp.minimum(i * tm // seq, nb), 0, 0)),
                  _resident(g3.shape), _resident(w.shape),
                  pl.BlockSpec((tm, 384), rope_idx)],
        out_specs=(pl.BlockSpec((tm, 512), lambda i: (i, 0)),
                   pl.BlockSpec((tm, 512), lambda i: (i, 0)),
                   pl.BlockSpec((tm, B_CH), lambda i: (i, 0))),
        compiler_params=_params(nbytes),
        name="ab_in_proj",
    )(x, mod, g3, w, rope_tab)


def _gqa_core(q_ref, kv_blocks, lane_ref, sink_ref, o_ref, mask):
    r = q_ref.shape[0]
    m_lo, m_hi = lane_ref[0:1, :], lane_ref[1:2, :]
    kv = jnp.concatenate(kv_blocks, axis=0) if len(kv_blocks) > 1 else kv_blocks[0]
    nk = kv.shape[0]
    k, k_sw, v, v_sw = kv[:, 0:128], kv[:, 128:256], kv[:, 256:384], kv[:, 384:512]
    f_lo, f_hi = m_lo.astype(F32), m_hi.astype(F32)
    for hk in range(A_KV_HEADS):
        k_lo, k_hi = (k * m_lo, k_sw * m_hi) if hk == 0 else (k_sw * m_lo, k * m_hi)
        v_lo, v_hi = (v * m_lo, v_sw * m_hi) if hk == 0 else (v_sw * m_lo, v * m_hi)
        qs = jnp.concatenate([q_ref[:, (2 * hk) * 128:(2 * hk + 1) * 128],
                              q_ref[:, (2 * hk + 1) * 128:(2 * hk + 2) * 128]], axis=0)
        kcat = jnp.concatenate([k_lo, k_hi], axis=0)
        vcat = jnp.concatenate([v_lo, v_hi], axis=0)
        s = lax.dot_general(qs, kcat, (((1,), (1,)), ((), ())), preferred_element_type=F32)
        ps, rinv = [], []
        for j in range(2):
            pj, rj = [], []
            for par in range(2):
                sj = s[j * r:(j + 1) * r, par * nk:(par + 1) * nk]
                if mask is not None:
                    sj = jnp.where(mask, sj, NEG_INF)
                sink = sink_ref[4 * hk + 2 * j + par] * LOG2E
                m = jnp.maximum(jnp.max(sj, axis=-1, keepdims=True), sink)
                e = jnp.exp2(sj - m)
                l = jnp.sum(e, axis=-1, keepdims=True) + jnp.exp2(sink - m)
                pj.append(e.astype(BF16))
                rj.append(1.0 / l)
            ps.append(jnp.concatenate(pj, axis=1))
            rinv.append(rj[0] * f_lo + rj[1] * f_hi)
        p = jnp.concatenate(ps, axis=0)
        o = jnp.dot(p, vcat, preferred_element_type=F32)
        for j in range(2):
            blk = 2 * hk + j
            o_ref[:, blk * 128:(blk + 1) * 128] = (o[j * r:(j + 1) * r] * rinv[j]).astype(BF16)


def _gqa_window_kernel(sink_ref, q_ref, kvp_ref, kvc_ref, kvn_ref, kvx_ref, lane_ref, o_ref, *, nblk):
    n = pl.program_id(1)
    r = q_ref.shape[0]
    ctx = kvx_ref.shape[0]
    nk = 3 * A_BLOCK + ctx
    i = lax.broadcasted_iota(jnp.int32, (r, nk), 0)
    j = lax.broadcasted_iota(jnp.int32, (r, nk), 1)
    lo = jnp.where(n > 0, 0, A_BLOCK)
    hi = jnp.where(n < nblk - 1, 3 * A_BLOCK, 2 * A_BLOCK)
    band = (j >= i) & (j <= i + 2 * A_WINDOW) & (j >= lo) & (j < hi)
    mask = band | (j >= 3 * A_BLOCK)
    _gqa_core(q_ref, [kvp_ref[...], kvc_ref[...], kvn_ref[...], kvx_ref[...]], lane_ref, sink_ref, o_ref, mask)


def _gqa_ctx_kernel(sink_ref, q_ref, kvx_ref, lane_ref, o_ref):
    _gqa_core(q_ref, [kvx_ref[...]], lane_ref, sink_ref, o_ref, None)


def _gqa(q, kv, sink, lane_masks, *, batch, seq, ctx):
    rows = q.shape[0]
    nblk = seq // A_BLOCK
    n_lat = batch * seq
    ctx_blk0 = n_lat // ctx
    smem = pl.BlockSpec(memory_space=pltpu.SMEM)
    lane_spec = lambda nd: pl.BlockSpec((2, 128), lambda *_: (0, 0))
    row_blk = lambda f: pl.BlockSpec((A_BLOCK, 512), f)
    nbytes = 2 * (5 * A_BLOCK + ctx) * 512 * 2 + 6 * (2 * A_BLOCK) * 2 * (3 * A_BLOCK + ctx) * 4
    o_lat = pl.pallas_call(
        functools.partial(_gqa_window_kernel, nblk=nblk),
        out_shape=jax.ShapeDtypeStruct((n_lat, 512), BF16),
        grid=(batch, nblk),
        in_specs=[smem,
                  row_blk(lambda b, n: (b * nblk + n, 0)),
                  row_blk(lambda b, n: (b * nblk + jnp.maximum(n - 1, 0), 0)),
                  row_blk(lambda b, n: (b * nblk + n, 0)),
                  row_blk(lambda b, n: (b * nblk + jnp.minimum(n + 1, nblk - 1), 0)),
                  pl.BlockSpec((ctx, 512), lambda b, n: (ctx_blk0 + b, 0)),
                  lane_spec(2)],
        out_specs=row_blk(lambda b, n: (b * nblk + n, 0)),
        compiler_params=_params(nbytes, 2),
        name="gqa_window",
    )(sink, q, kv, kv, kv, kv, lane_masks)
    nbytes_c = 2 * 3 * ctx * 512 * 2 + 6 * (2 * ctx) * (2 * ctx) * 4
    o_ctx = pl.pallas_call(
        _gqa_ctx_kernel,
        out_shape=jax.ShapeDtypeStruct((rows - n_lat, 512), BF16),
        grid=(batch,),
        in_specs=[smem,
                  pl.BlockSpec((ctx, 512), lambda b: (ctx_blk0 + b, 0)),
                  pl.BlockSpec((ctx, 512), lambda b: (ctx_blk0 + b, 0)),
                  lane_spec(1)],
        out_specs=pl.BlockSpec((ctx, 512), lambda b: (b, 0)),
        compiler_params=_params(nbytes_c, 1),
        name="gqa_context",
    )(sink, q, kv, lane_masks)
    return o_lat, o_ctx


def _conv_kernel(yp_ref, yc_ref, yn_ref, w_ref, b_ref, g_ref, beta_ref, o_ref, ext_ref, acc_ref, *,
                 n_lat_tiles, lat_tiles_per_seq, ctx_tiles_per_seq):
    i = pl.program_id(0)
    tm, ch = yc_ref.shape
    is_lat = i < n_lat_tiles
    pos = jnp.where(is_lat, i % lat_tiles_per_seq, (i - n_lat_tiles) % ctx_tiles_per_seq)
    last_pos = jnp.where(is_lat, lat_tiles_per_seq - 1, ctx_tiles_per_seq - 1)
    keep_prev = (pos > 0).astype(F32)
    keep_next = (pos < last_pos).astype(F32)
    ext_ref[0:CONV_HALO, :] = yp_ref[...] * keep_prev
    ext_ref[CONV_HALO:CONV_HALO + tm, :] = yc_ref[...]
    ext_ref[CONV_HALO + tm:, :] = yn_ref[...] * keep_next
    off = CONV_HALO - B_KERNEL // 2
    sub = V7X_SUBLANES
    n_a = -(-(B_KERNEL + off) // sub)
    rc = min(CONV_ROW_CHUNK, tm)
    for cb in range(ch // V7X_LANES):
        cs = slice(cb * V7X_LANES, (cb + 1) * V7X_LANES)
        for r0 in range(0, tm, rc):
            acc = None
            for s in range(sub):
                z = None
                for a in range(n_a):
                    j = sub * a + s - off
                    if 0 <= j < B_KERNEL:
                        term = w_ref[j:j + 1, cs] * ext_ref[r0 + sub * a:r0 + sub * a + rc + sub, cs]
                        z = term if z is None else z + term
                zs = z[s:s + rc, :]
                acc = zs if acc is None else acc + zs
            acc_ref[r0:r0 + rc, cs] = acc + b_ref[:, cs]
    y = acc_ref[...]
    mu = jnp.mean(y, axis=-1, keepdims=True)
    yc = y - mu
    var = jnp.mean(yc * yc, axis=-1, keepdims=True)
    z = yc * lax.rsqrt(var + LN_EPS) * g_ref[...] + beta_ref[...]
    o_ref[...] = (z * jax.nn.sigmoid(z)).astype(BF16)


def _conformer_conv(y, w_dw, b_dw, ln_g, ln_b, *, tm, seq, ctx, n_lat_rows):
    rows, ch = y.shape
    hb = tm // CONV_HALO
    n_halo_blocks = rows // CONV_HALO
    kern = functools.partial(_conv_kernel, n_lat_tiles=n_lat_rows // tm,
                             lat_tiles_per_seq=seq // tm, ctx_tiles_per_seq=ctx // tm)
    nbytes = 2 * (tm + 2 * CONV_HALO) * ch * 4 + 2 * tm * ch * 2 + (2 * tm + 2 * CONV_HALO) * ch * 4 + 6 * tm * ch * 4
    vec = lambda: pl.BlockSpec((1, ch), lambda i: (0, 0))
    return pl.pallas_call(
        kern,
        out_shape=jax.ShapeDtypeStruct((rows, ch), BF16),
        grid=(rows // tm,),
        in_specs=[pl.BlockSpec((CONV_HALO, ch), lambda i: (jnp.maximum(i * hb - 1, 0), 0)),
                  pl.BlockSpec((tm, ch), lambda i: (i, 0)),
                  pl.BlockSpec((CONV_HALO, ch), lambda i: (jnp.minimum((i + 1) * hb, n_halo_blocks - 1), 0)),
                  pl.BlockSpec((B_KERNEL, ch), lambda i: (0, 0)),
                  vec(), vec(), vec()],
        out_specs=pl.BlockSpec((tm, ch), lambda i: (i, 0)),
        scratch_shapes=[pltpu.VMEM((tm + 2 * CONV_HALO, ch), F32), pltpu.VMEM((tm, ch), F32)],
        compiler_params=_params(nbytes),
        name="conformer_conv",
    )(y, y, y, w_dw, b_dw.reshape(1, ch), ln_g.reshape(1, ch), ln_b.reshape(1, ch))


def _proj1_kernel(x_ref, mod_ref, g_ref, w1_ref, gq_ref, gkv_ref, wq_ref, wk_ref, wvt_ref, rope_ref,
                  q_ref, k_ref, vt_ref, *, q_scale):
    mod = mod_ref[0]
    xm = (_rms(x_ref[...], g_ref[1:2]) * (1.0 + mod[4:5]) + mod[3:4]).astype(BF16)
    c1 = jnp.dot(xm, w1_ref[...], preferred_element_type=F32)
    cos, s_next, s_prev = rope_ref[:, 0:128], rope_ref[:, 128:256], rope_ref[:, 256:384]
    cq = _rms(c1[:, :C_Q_LORA], gq_ref[...] * q_scale).astype(BF16)
    ckv = _rms(c1[:, C_Q_LORA:C_Q_LORA + C_KV_LORA], gkv_ref[...]).astype(BF16)
    kr = _rope128(c1[:, C_Q_LORA + C_KV_LORA:], cos, s_next, s_prev).astype(BF16)
    q = jnp.dot(cq, wq_ref[...], preferred_element_type=F32)
    kn = jnp.dot(ckv, wk_ref[...], preferred_element_type=F32)
    hp = C_HEAD_PAD
    for h in range(C_HEADS):
        q_ref[:, h * hp:h * hp + C_NOPE] = q[:, h * hp:h * hp + C_NOPE].astype(BF16)
        q_ref[:, h * hp + C_NOPE:(h + 1) * hp] = _rope128(
            q[:, h * hp + C_NOPE:(h + 1) * hp], cos, s_next, s_prev).astype(BF16)
        k_ref[:, h * hp:h * hp + C_NOPE] = kn[:, h * C_NOPE:(h + 1) * C_NOPE].astype(BF16)
        k_ref[:, h * hp + C_NOPE:(h + 1) * hp] = kr
    vt_ref[...] = lax.dot_general(wvt_ref[...], ckv, (((1,), (1,)), ((), ())),
                                  preferred_element_type=F32).astype(BF16)


def _proj1(x, mod, g3, w1, gq, gkv, wq, wk, wvt, rope_tab, *, tm, seq, n_lat_rows):
    rows, d = x.shape
    nb = mod.shape[0] - 1
    n_lat_tiles = n_lat_rows // tm
    tiles_per_seq = seq // tm
    rope_idx = lambda i: (jnp.where(i < n_lat_tiles, i % tiles_per_seq, tiles_per_seq), 0)
    qw = C_HEADS * C_HEAD_PAD
    vw = C_HEADS * C_V
    nbytes = (2 * (w1.size + wq.size + wk.size + wvt.size) + 2 * tm * (d * 4 + 384 * 4 + (2 * qw + vw) * 2)
              + tm * (w1.shape[1] + wq.shape[1] + wk.shape[1] + vw) * 4 + tm * d * 6)
    row = lambda w: pl.BlockSpec((tm, w), lambda i: (i, 0))
    return pl.pallas_call(
        functools.partial(_proj1_kernel, q_scale=(C_NOPE + C_ROPE) ** -0.5 * LOG2E),
        out_shape=(jax.ShapeDtypeStruct((rows, qw), BF16),
                   jax.ShapeDtypeStruct((rows, qw), BF16),
                   jax.ShapeDtypeStruct((vw, rows), BF16)),
        grid=(rows // tm,),
        in_specs=[row(d),
                  pl.BlockSpec((1, N_MOD, d), lambda i: (jnp.minimum(i * tm // seq, nb), 0, 0)),
                  _resident(g3.shape), _resident(w1.shape), _resident(gq.shape), _resident(gkv.shape),
                  _resident(wq.shape), _resident(wk.shape), _resident(wvt.shape),
                  pl.BlockSpec((tm, 384), rope_idx)],
        out_specs=(row(qw), row(qw), pl.BlockSpec((vw, tm), lambda i: (0, i))),
        compiler_params=_params(nbytes),
        name="mla_proj",
    )(x, mod, g3, w1, gq, gkv, wq, wk, wvt, rope_tab)


def _col_reduce(s, op, chains=8):
    r, n = s.shape
    s3 = s.reshape(r // V7X_SUBLANES, V7X_SUBLANES, n)
    chains = min(chains, s3.shape[0])
    step = s3.shape[0] // chains
    parts = []
    for g in range(chains):
        acc = s3[g * step]
        for i in range(g * step + 1, (g + 1) * step if g < chains - 1 else s3.shape[0]):
            acc = op(acc, s3[i])
        parts.append(acc)
    while len(parts) > 1:
        parts = [op(parts[i], parts[i + 1]) for i in range(0, len(parts) - 1, 2)] + (
            [parts[-1]] if len(parts) % 2 else [])
    return parts[0]


def _mla_kernel(q_ref, kl_ref, kc_ref, vtl_ref, vtc_ref, o_ref):
    hp = C_HEAD_PAD
    dn = (((1,), (1,)), ((), ()))

    def scores(h):
        qh = q_ref[:, h * hp:(h + 1) * hp]
        return (lax.dot_general(kl_ref[:, h * hp:(h + 1) * hp], qh, dn, preferred_element_type=F32),
                lax.dot_general(kc_ref[:, h * hp:(h + 1) * hp], qh, dn, preferred_element_type=F32))

    nxt = scores(0)
    for h in range(C_HEADS):
        s1, s2 = nxt
        if h + 1 < C_HEADS:
            nxt = scores(h + 1)
        m = jnp.max(jnp.maximum(_col_reduce(s1, jnp.maximum), _col_reduce(s2, jnp.maximum)),
                    axis=0, keepdims=True)
        e1 = jnp.exp2(s1 - m)
        e2 = jnp.exp2(s2 - m)
        l = jnp.sum(_col_reduce(e1, jnp.add) + _col_reduce(e2, jnp.add), axis=0, keepdims=True)
        ot = (jnp.dot(vtl_ref[h * C_V:(h + 1) * C_V, :], e1.astype(BF16), preferred_element_type=F32)
              + jnp.dot(vtc_ref[h * C_V:(h + 1) * C_V, :], e2.astype(BF16), preferred_element_type=F32))
        o_ref[:, h * C_V:(h + 1) * C_V] = (ot * (1.0 / l)).T.astype(BF16)


def _mla_attention(q, k, vt, *, batch, seq, ctx, tq):
    n_lat = batch * seq
    qw, vw = k.shape[1], vt.shape[0]
    nq = seq // tq
    ctx_blk0 = n_lat // ctx
    nbytes = (2 * (seq + ctx) * (qw + vw) * 2 + 2 * tq * (qw + vw) * 2 + 4 * tq * (seq + ctx) * 4)
    return pl.pallas_call(
        _mla_kernel,
        out_shape=jax.ShapeDtypeStruct((n_lat, vw), BF16),
        grid=(batch, nq),
        in_specs=[pl.BlockSpec((tq, qw), lambda b, i: (b * nq + i, 0)),
                  pl.BlockSpec((seq, qw), lambda b, i: (b, 0)),
                  pl.BlockSpec((ctx, qw), lambda b, i: (ctx_blk0 + b, 0)),
                  pl.BlockSpec((vw, seq), lambda b, i: (0, b)),
                  pl.BlockSpec((vw, ctx), lambda b, i: (0, ctx_blk0 + b))],
        out_specs=pl.BlockSpec((tq, vw), lambda b, i: (b * nq + i, 0)),
        compiler_params=_params(nbytes, 2),
        name="mla_attention",
    )(q, k, k, vt, vt)


def _rope_table(seq, pad_rows):
    t = jnp.arange(seq)
    row = (t // GRID_W).astype(F32)
    col = (t % GRID_W).astype(F32)
    d_axis = A_HEAD_DIM // 2
    inv_freq = ROPE_BASE ** (-jnp.arange(0, d_axis, 2, dtype=F32) / d_axis)
    ang_r = row[:, None] * inv_freq
    ang_c = col[:, None] * inv_freq
    ang = jnp.concatenate([ang_r, ang_r, ang_c, ang_c], axis=-1)
    cos, sin = jnp.cos(ang), jnp.sin(ang)
    first_half = (jnp.arange(A_HEAD_DIM) % 32) < 16
    s_next = jnp.where(first_half, -sin, 0.0)
    s_prev = jnp.where(first_half, 0.0, sin)
    tab = jnp.concatenate([jnp.tile(cos, (1, 2)), jnp.tile(s_next, (1, 2)), jnp.tile(s_prev, (1, 2))], axis=-1)
    ident = jnp.concatenate([jnp.ones((pad_rows, 128), F32), jnp.zeros((pad_rows, 256), F32)], axis=-1)
    return jnp.concatenate([tab, ident], axis=0)


def kernel(x, c, ctx, c_ctx, w_mod, b_mod, g_norm, ffn_w_in, ffn_w_out, ab_w_in, a_sink, b_w_dw, b_b_dw,
           b_ln_g, b_ln_b, ab_w_out, c_w_dq, c_g_q, c_w_uq, c_w_dkv, c_g_kv, c_w_uk, c_w_uv, c_w_o, g_final):
    batch, seq, d = x.shape
    ctx_len = ctx.shape[1]
    depth = w_mod.shape[0]
    d_ff = ffn_w_out.shape[2]
    n_lat = batch * seq
    rows = n_lat + batch * ctx_len
    tm = min(512, seq)
    tm_conv = min(256, ctx_len)
    tq = min(2 * V7X_MXU_DIM, seq)
    ck = V7X_MXU_DIM
    assert seq % tm == 0 and (batch * ctx_len) % tm == 0 and seq % A_BLOCK == 0 and seq % GRID_W == 0
    assert ctx_len % tm_conv == 0 and seq % tm_conv == 0 and n_lat % ctx_len == 0 and d_ff % ck == 0
    assert depth % 2 == 0 and depth <= 2

    r_mod = -(-(batch + 1) // 16) * 16
    cc = jnp.concatenate([c, c_ctx[None, :], jnp.zeros((r_mod - batch - 1, d), F32)], axis=0)
    mod_all = _modulation(cc, w_mod, b_mod)[:, :batch + 1].reshape(depth, batch + 1, N_MOD, d)

    rope_tab = _rope_table(seq, tm)
    lane_masks = (jnp.arange(128)[None, :] // 64 == jnp.arange(2)[:, None]).astype(BF16)
    t = (x.reshape(n_lat, d), ctx.reshape(batch * ctx_len, d))

    for i in range(depth):
        last = i == depth - 1
        j = i // 2
        mod = mod_all[i]
        g3 = g_norm[i]
        t = _ffn(t, mod, g3, ffn_w_in[i, 0].astype(BF16), ffn_w_out[i, 0].astype(BF16), rows=rows, tm=tm, seq=seq, n_lat_rows=n_lat,
                 sub=0, ck=ck)
        if i % 2 == 0:
            w = ab_w_in[j]
            iq, ik = A_HEADS * A_HEAD_DIM, A_KV_HEADS * A_HEAD_DIM
            wk, wv = w[:, iq:iq + ik], w[:, iq + ik:iq + 2 * ik]
            swap = lambda m: jnp.concatenate([m[:, A_HEAD_DIM:], m[:, :A_HEAD_DIM]], axis=1)
            w_all = jnp.concatenate([w[:, :iq], wk, swap(wk), wv, swap(wv), w[:, iq + 2 * ik:]], axis=1).astype(BF16)
            q, kv, y = _proj0(t, mod, g3, w_all, rope_tab, tm=tm, seq=seq, n_lat_rows=n_lat)
            a = _gqa(q, kv, a_sink[j].astype(F32), lane_masks, batch=batch, seq=seq, ctx=ctx_len)
            bx = _conformer_conv(y, b_w_dw[j], b_b_dw[j], b_ln_g[j], b_ln_b[j],
                                 tm=tm_conv, seq=seq, ctx=ctx_len, n_lat_rows=n_lat)
            wo = ab_w_out[j].astype(BF16)
            attn, wos = (a, bx), (wo[:iq], wo[iq:])
        else:
            hp = C_HEAD_PAD
            wdkv = c_w_dkv[j]
            w1 = jnp.concatenate([c_w_dq[j], wdkv, jnp.zeros((d, 128 - C_ROPE), F32)], axis=1).astype(BF16)
            wuq = c_w_uq[j].reshape(C_Q_LORA, C_HEADS, C_NOPE + C_ROPE)
            wuq = jnp.pad(wuq, ((0, 0), (0, 0), (0, hp - C_NOPE - C_ROPE))).reshape(C_Q_LORA, C_HEADS * hp)
            q, k, vt = _proj1(t, mod, g3, w1, c_g_q[j].reshape(1, -1), c_g_kv[j].reshape(1, -1),
                              wuq.astype(BF16), c_w_uk[j].astype(BF16), c_w_uv[j].T.astype(BF16), rope_tab,
                              tm=tm, seq=seq, n_lat_rows=n_lat)
            a = _mla_attention(q, k, vt, batch=batch, seq=seq, ctx=ctx_len, tq=tq)
            attn, wos = (a,), (c_w_o[j].astype(BF16),)
        t = _ffn(t, mod, g3, ffn_w_in[i, 1].astype(BF16), ffn_w_out[i, 1].astype(BF16), rows=n_lat if last else rows, tm=tm, seq=seq,
                 n_lat_rows=n_lat, sub=2, ck=ck, attn=attn, wo=wos, g_final=g_final if last else None)
    return t.reshape(batch, seq, d)
```

```python
import functools
import math

import jax
import jax.numpy as jnp
from jax import lax
from jax.experimental import pallas as pl
from jax.experimental.pallas import tpu as pltpu

F32 = jnp.float32
BF16 = jnp.bfloat16

GRID_W = 64
ROPE_BASE = 10000.0
NORM_EPS = 1e-6
LN_EPS = 1e-5
NEG_INF = -1e30
LOG2E = math.log2(math.e)
N_MOD = 9
FFN_RES = 0.5
A_HEADS = 8
A_KV_HEADS = 2
A_HEAD_DIM = 64
A_WINDOW = 128
A_BLOCK = 128
B_CH = 512
B_KERNEL = 31
C_HEADS = 8
C_Q_LORA = 256
C_KV_LORA = 256
C_NOPE = 128
C_ROPE = 64
C_V = 128

V7X_LANES = 128
V7X_SUBLANES = 8
V7X_MXU_DIM = 256
V7X_VMEM_BYTES = 64 * 1024 * 1024
V7X_VMEM_USABLE = 56 * 1024 * 1024

C_HEAD_PAD = V7X_MXU_DIM
CONV_HALO = 16


def _vmem_limit(nbytes):
    return int(min(V7X_VMEM_USABLE, max(16 * 1024 * 1024, nbytes * 3 // 2)))


def _params(nbytes, ngrid=1):
    return pltpu.CompilerParams(dimension_semantics=("arbitrary",) * ngrid,
                                vmem_limit_bytes=_vmem_limit(nbytes))


def _rms(x, g):
    return x * lax.rsqrt(jnp.mean(x * x, axis=-1, keepdims=True) + NORM_EPS) * g


def _resident(shape, lead=()):
    nd = len(shape) - len(lead)
    return pl.BlockSpec((None,) * len(lead) + tuple(shape[len(lead):]),
                        lambda *_: tuple(lead) + (0,) * nd, pipeline_mode=pl.Buffered(1))


def _mod_kernel(c_ref, w_ref, b_ref, o_ref):
    c = c_ref[...]
    a = (c * jax.nn.sigmoid(c)).astype(BF16)
    o_ref[0] = jnp.dot(a, w_ref[0].astype(BF16), preferred_element_type=F32) + b_ref[0]


def _modulation(cc, w_mod, b_mod):
    depth, d, n = w_mod.shape
    r = cc.shape[0]
    tn = n // N_MOD
    nbytes = 2 * (d * tn * 4) + d * tn * 2 + 4 * r * (d + 2 * tn) * 4
    return pl.pallas_call(
        _mod_kernel,
        out_shape=jax.ShapeDtypeStruct((depth, r, n), F32),
        grid=(depth, n // tn),
        in_specs=[pl.BlockSpec((r, d), lambda l, j: (0, 0)),
                  pl.BlockSpec((1, d, tn), lambda l, j: (l, 0, j)),
                  pl.BlockSpec((1, 1, tn), lambda l, j: (l, 0, j))],
        out_specs=pl.BlockSpec((1, r, tn), lambda l, j: (l, 0, j)),
        compiler_params=_params(nbytes, 2),
        name="adaln_modulation",
    )(cc, w_mod, b_mod.reshape(depth, 1, n))


def _row_specs(op, tm, n_lat_tiles):
    if not isinstance(op, tuple):
        return [pl.BlockSpec((tm, op.shape[1]), lambda i: (i, 0))]
    lat, cx = op
    return [pl.BlockSpec((tm, lat.shape[1]), lambda i: (jnp.minimum(i, n_lat_tiles - 1), 0)),
            pl.BlockSpec((tm, cx.shape[1]), lambda i: (jnp.maximum(i - n_lat_tiles, 0), 0))]


def _row_load(refs, is_lat):
    if len(refs) == 1:
        return refs[0][...]
    return jnp.where(is_lat, refs[0][...], refs[1][...])


def _ffn_kernel(*refs, arity, sub, final_norm, ck, n_lat_tiles):
    refs = list(refs)
    o_ref = refs.pop()
    take = lambda n: [refs.pop(0) for _ in range(n)]
    x_refs = take(arity[0])
    mod_ref, g_ref = take(2)
    a_refs = [take(n) for n in arity[1:]]
    wo_refs = take(len(arity) - 1)
    win_ref, wout_ref = take(2)
    gf_ref = refs.pop(0) if final_norm else None
    is_lat = pl.program_id(0) < n_lat_tiles

    x = _row_load(x_refs, is_lat)
    mod = mod_ref[0]
    if a_refs:
        ox = None
        for ar, wo_ref in zip(a_refs, wo_refs):
            part = jnp.dot(_row_load(ar, is_lat), wo_ref[...], preferred_element_type=F32)
            ox = part if ox is None else ox + part
        x = x + mod[5:6] * ox
    shift, scale, gate = mod[3 * sub:3 * sub + 1], mod[3 * sub + 1:3 * sub + 2], mod[3 * sub + 2:3 * sub + 3]
    h = (_rms(x, g_ref[sub:sub + 1]) * (1.0 + scale) + shift).astype(BF16)
    d_ff = wout_ref.shape[0]
    acc = None
    for c in range(d_ff // ck):
        g = jnp.dot(h, win_ref[:, c * ck:(c + 1) * ck], preferred_element_type=F32)
        u = jnp.dot(h, win_ref[:, d_ff + c * ck:d_ff + (c + 1) * ck], preferred_element_type=F32)
        act = (g * jax.nn.sigmoid(g) * u).astype(BF16)
        part = jnp.dot(act, wout_ref[c * ck:(c + 1) * ck, :], preferred_element_type=F32)
        acc = part if acc is None else acc + part
    y = x + (FFN_RES * gate) * acc
    if final_norm:
        y = _rms(y, gf_ref[...])
    o_ref[...] = y


def _ffn(x, mod, g3, win, wout, widx, *, rows, tm, seq, n_lat_rows, sub, ck, attn=(), wo=(), g_final=None):
    d = win.shape[-2]
    nb = mod.shape[0] - 1
    n_lat_tiles = n_lat_rows // tm
    arity = tuple(2 if isinstance(o, tuple) else 1 for o in (x, *attn))
    flat = lambda o: list(o) if isinstance(o, tuple) else [o]
    in_specs = _row_specs(x, tm, n_lat_tiles)
    in_specs += [pl.BlockSpec((1, N_MOD, d), lambda i: (jnp.minimum(i * tm // seq, nb), 0, 0)),
                 _resident(g3.shape)]
    for a in attn:
        in_specs += _row_specs(a, tm, n_lat_tiles)
    in_specs += [_resident(w.shape) for w in wo]
    in_specs += [_resident(win.shape, widx), _resident(wout.shape, widx)]
    args = [*flat(x), mod, g3, *[m for a in attn for m in flat(a)], *wo, win, wout]
    if g_final is not None:
        in_specs.append(_resident((1, d)))
        args.append(g_final.reshape(1, d))
    weights = 2 * (math.prod(win.shape[-2:]) + math.prod(wout.shape[-2:]) + sum(w.size for w in wo))
    tiles = 2 * tm * (arity[0] * d * 4 + d * 4 + sum(n * flat(a)[0].shape[1] for n, a in zip(arity[1:], attn)) * 2)
    temps = tm * (d * 4 * 3 + d * 2 + 2 * ck * 4 * 2 + ck * 2)
    kern = functools.partial(_ffn_kernel, arity=arity, sub=sub, final_norm=g_final is not None,
                             ck=ck, n_lat_tiles=n_lat_tiles)
    return pl.pallas_call(
        kern,
        out_shape=jax.ShapeDtypeStruct((rows, d), F32),
        grid=(rows // tm,),
        in_specs=in_specs,
        out_specs=pl.BlockSpec((tm, d), lambda i: (i, 0)),
        compiler_params=_params(weights + tiles + temps),
        name=f"ffn_sub{sub}" + ("_mix" if attn else "") + ("_final" if g_final is not None else ""),
    )(*args)


def _rope128(v, cos, sin_next, sin_prev):
    return (v * cos + pltpu.roll(v, V7X_LANES - 16, 1) * sin_next + pltpu.roll(v, 16, 1) * sin_prev)


def _proj0_kernel(x_ref, mod_ref, g_ref, w_ref, rope_ref, q_ref, kv_ref, y_ref, *, q_scale):
    mod = mod_ref[0]
    xm = (_rms(x_ref[...], g_ref[1:2]) * (1.0 + mod[4:5]) + mod[3:4]).astype(BF16)
    p = jnp.dot(xm, w_ref[...], preferred_element_type=F32)
    cos, s_next, s_prev = rope_ref[:, 0:128], rope_ref[:, 128:256], rope_ref[:, 256:384]
    nq = A_HEADS * A_HEAD_DIM // V7X_LANES
    for j in range(nq):
        blk = p[:, j * 128:(j + 1) * 128]
        q_ref[:, j * 128:(j + 1) * 128] = (_rope128(blk, cos, s_next, s_prev) * q_scale).astype(BF16)
    base = nq * 128
    for j in range(2):
        blk = p[:, base + j * 128:base + (j + 1) * 128]
        kv_ref[:, j * 128:(j + 1) * 128] = _rope128(blk, cos, s_next, s_prev).astype(BF16)
    kv_ref[:, 256:512] = p[:, base + 256:base + 512].astype(BF16)
    a = p[:, base + 512:base + 512 + B_CH]
    g = p[:, base + 512 + B_CH:base + 512 + 2 * B_CH]
    y_ref[...] = a * jax.nn.sigmoid(g)


def _proj0(x, mod, g3, w, rope_tab, *, tm, seq, n_lat_rows):
    rows, d = x.shape
    nb = mod.shape[0] - 1
    n_lat_tiles = n_lat_rows // tm
    tiles_per_seq = seq // tm
    rope_idx = lambda i: (jnp.where(i < n_lat_tiles, i % tiles_per_seq, tiles_per_seq), 0)
    nw = w.shape[1]
    nbytes = 2 * w.size + 2 * tm * (d * 4 + 384 * 4 + 1024 * 2 + 512 * 4) + tm * (nw * 4 + d * 6)
    return pl.pallas_call(
        functools.partial(_proj0_kernel, q_scale=A_HEAD_DIM ** -0.5 * LOG2E),
        out_shape=(jax.ShapeDtypeStruct((rows, 512), BF16),
                   jax.ShapeDtypeStruct((rows, 512), BF16),
                   jax.ShapeDtypeStruct((rows, B_CH), F32)),
        grid=(rows // tm,),
        in_specs=[pl.BlockSpec((tm, d), lambda i: (i, 0)),
                  pl.BlockSpec((1, N_MOD, d), lambda i: (jnp.minimum(i * tm // seq, nb), 0, 0)),
                  _resident(g3.shape), _resident(w.shape),
                  pl.BlockSpec((tm, 384), rope_idx)],
        out_specs=(pl.BlockSpec((tm, 512), lambda i: (i, 0)),
                   pl.BlockSpec((tm, 512), lambda i: (i, 0)),
                   pl.BlockSpec((tm, B_CH), lambda i: (i, 0))),
        compiler_params=_params(nbytes),
        name="ab_in_proj",
    )(x, mod, g3, w, rope_tab)


def _gqa_core(q_ref, kv_blocks, groups, lane_ref, sink_ref, o_ref):
    m_lo, m_hi = lane_ref[0:1, :], lane_ref[1:2, :]
    f_lo, f_hi = m_lo.astype(F32), m_hi.astype(F32)
    k_var, v_var = [], []
    for blk in kv_blocks:
        k, k_sw, v, v_sw = blk[:, 0:128], blk[:, 128:256], blk[:, 256:384], blk[:, 384:512]
        k_var.append(((k * m_lo, k_sw * m_hi), (k_sw * m_lo, k * m_hi)))
        v_var.append(((v * m_lo, v_sw * m_hi), (v_sw * m_lo, v * m_hi)))
    chains = [(g, hk) for g in groups for hk in range(A_KV_HEADS)]

    def scores(group, hk):
        row0, r, ids, _ = group
        qs = jnp.concatenate([q_ref[row0:row0 + r, (2 * hk) * 128:(2 * hk + 1) * 128],
                              q_ref[row0:row0 + r, (2 * hk + 1) * 128:(2 * hk + 2) * 128]], axis=0)
        kcat = jnp.concatenate([k_var[b][hk][0] for b in ids] + [k_var[b][hk][1] for b in ids], axis=0)
        return lax.dot_general(qs, kcat, (((1,), (1,)), ((), ())), preferred_element_type=F32)

    all_scores = [scores(g, hk) for g, hk in chains]
    for (group, hk), s in zip(chains, all_scores):
        row0, r, ids, mask = group
        nk = s.shape[1] // 2
        vcat = jnp.concatenate([v_var[b][hk][0] for b in ids] + [v_var[b][hk][1] for b in ids], axis=0)
        ps, rinv = [], []
        for j in range(2):
            pj, rj = [], []
            for par in range(2):
                sj = s[j * r:(j + 1) * r, par * nk:(par + 1) * nk]
                if mask is not None:
                    sj = jnp.where(mask, sj, NEG_INF)
                sink = sink_ref[4 * hk + 2 * j + par] * LOG2E
                m = jnp.maximum(jnp.max(sj, axis=-1, keepdims=True), sink)
                e = jnp.exp2(sj - m)
                l = jnp.sum(e, axis=-1, keepdims=True) + jnp.exp2(sink - m)
                pj.append(e.astype(BF16))
                rj.append(1.0 / l)
            ps.append(jnp.concatenate(pj, axis=1))
            rinv.append(rj[0] * f_lo + rj[1] * f_hi)
        p = jnp.concatenate(ps, axis=0)
        o = jnp.dot(p, vcat, preferred_element_type=F32)
        for j in range(2):
            blk = 2 * hk + j
            o_ref[row0:row0 + r, blk * 128:(blk + 1) * 128] = (o[j * r:(j + 1) * r] * rinv[j]).astype(BF16)


def _gqa_window_kernel(sink_ref, q_ref, kvp_ref, kvc_ref, kvn_ref, kvx_ref, lane_ref, o_ref, *, npair):
    n = pl.program_id(1)
    ctx = kvx_ref.shape[0]
    nk = 3 * A_BLOCK + ctx
    i = lax.broadcasted_iota(jnp.int32, (A_BLOCK, nk), 0)
    j = lax.broadcasted_iota(jnp.int32, (A_BLOCK, nk), 1)
    band = (j >= i) & (j <= i + 2 * A_WINDOW)
    is_ctx = j >= 3 * A_BLOCK
    lo = jnp.where(n > 0, 0, A_BLOCK)
    hi = jnp.where(n < npair - 1, 3 * A_BLOCK, 2 * A_BLOCK)
    mask0 = (band & (j >= lo)) | is_ctx
    mask1 = (band & (j < hi)) | is_ctx
    cur = kvc_ref[...]
    blocks = [kvp_ref[...], cur[:A_BLOCK], cur[A_BLOCK:], kvn_ref[...], kvx_ref[...]]
    groups = [(0, A_BLOCK, (0, 1, 2, 4), mask0), (A_BLOCK, A_BLOCK, (1, 2, 3, 4), mask1)]
    _gqa_core(q_ref, blocks, groups, lane_ref, sink_ref, o_ref)


def _gqa_ctx_kernel(sink_ref, q_ref, kvx_ref, lane_ref, o_ref):
    _gqa_core(q_ref, [kvx_ref[...]], [(0, q_ref.shape[0], (0,), None)], lane_ref, sink_ref, o_ref)


def _gqa(q, kv, sink, lane_masks, *, batch, seq, ctx):
    rows = q.shape[0]
    nblk = seq // A_BLOCK
    n_lat = batch * seq
    ctx_blk0 = n_lat // ctx
    smem = pl.BlockSpec(memory_space=pltpu.SMEM)
    lane_spec = lambda nd: pl.BlockSpec((2, 128), lambda *_: (0, 0))
    npair = nblk // 2
    one_blk = lambda f: pl.BlockSpec((A_BLOCK, 512), f)
    two_blk = lambda f: pl.BlockSpec((2 * A_BLOCK, 512), f)
    nbytes = 2 * (6 * A_BLOCK + ctx) * 512 * 2 + 10 * (2 * A_BLOCK) * 2 * (3 * A_BLOCK + ctx) * 4
    o_lat = pl.pallas_call(
        functools.partial(_gqa_window_kernel, npair=npair),
        out_shape=jax.ShapeDtypeStruct((n_lat, 512), BF16),
        grid=(batch, npair),
        in_specs=[smem,
                  two_blk(lambda b, n: (b * npair + n, 0)),
                  one_blk(lambda b, n: (b * nblk + jnp.maximum(2 * n - 1, 0), 0)),
                  two_blk(lambda b, n: (b * npair + n, 0)),
                  one_blk(lambda b, n: (b * nblk + jnp.minimum(2 * n + 2, nblk - 1), 0)),
                  pl.BlockSpec((ctx, 512), lambda b, n: (ctx_blk0 + b, 0)),
                  lane_spec(2)],
        out_specs=two_blk(lambda b, n: (b * npair + n, 0)),
        compiler_params=_params(nbytes, 2),
        name="gqa_window",
    )(sink, q, kv, kv, kv, kv, lane_masks)
    nbytes_c = 2 * 3 * ctx * 512 * 2 + 6 * (2 * ctx) * (2 * ctx) * 4
    o_ctx = pl.pallas_call(
        _gqa_ctx_kernel,
        out_shape=jax.ShapeDtypeStruct((rows - n_lat, 512), BF16),
        grid=(batch,),
        in_specs=[smem,
                  pl.BlockSpec((ctx, 512), lambda b: (ctx_blk0 + b, 0)),
                  pl.BlockSpec((ctx, 512), lambda b: (ctx_blk0 + b, 0)),
                  lane_spec(1)],
        out_specs=pl.BlockSpec((ctx, 512), lambda b: (b, 0)),
        compiler_params=_params(nbytes_c, 1),
        name="gqa_context",
    )(sink, q, kv, lane_masks)
    return o_lat, o_ctx


def _conv_kernel(yp_ref, yc_ref, yn_ref, w_ref, b_ref, g_ref, beta_ref, o_ref, ext_ref, acc_ref, *,
                 n_lat_tiles, lat_tiles_per_seq, ctx_tiles_per_seq):
    i = pl.program_id(0)
    tm, ch = yc_ref.shape
    is_lat = i < n_lat_tiles
    pos = jnp.where(is_lat, i % lat_tiles_per_seq, (i - n_lat_tiles) % ctx_tiles_per_seq)
    last_pos = jnp.where(is_lat, lat_tiles_per_seq - 1, ctx_tiles_per_seq - 1)
    keep_prev = (pos > 0).astype(F32)
    keep_next = (pos < last_pos).astype(F32)
    n_ext = tm + 2 * CONV_HALO
    pitch = n_ext // V7X_SUBLANES
    half = B_KERNEL // 2
    pad = CONV_HALO
    zero = jnp.zeros((pad, V7X_LANES), F32)
    for cb in range(ch // V7X_LANES):
        cs = slice(cb * V7X_LANES, (cb + 1) * V7X_LANES)
        ext_ref[cb, 0:pad, :] = zero
        ext_ref[cb, pad:pad + CONV_HALO, :] = yp_ref[:, cs] * keep_prev
        ext_ref[cb, pad + CONV_HALO:pad + CONV_HALO + tm, :] = yc_ref[:, cs]
        ext_ref[cb, pad + CONV_HALO + tm:pad + n_ext, :] = yn_ref[:, cs] * keep_next
        ext_ref[cb, pad + n_ext:, :] = jnp.zeros((ext_ref.shape[1] - pad - n_ext, V7X_LANES), F32)
    for cb in range(ch // V7X_LANES):
        cs = slice(cb * V7X_LANES, (cb + 1) * V7X_LANES)
        for k in range(pitch):
            acc = None
            for t in range(B_KERNEL):
                term = w_ref[t:t + 1, cs] * ext_ref[cb, pl.ds(pad + k + t - half, V7X_SUBLANES, stride=pitch), :]
                acc = term if acc is None else acc + term
            acc_ref[cb, pl.ds(k, V7X_SUBLANES, stride=pitch), :] = acc + b_ref[:, cs]
    y = jnp.concatenate([acc_ref[cb, CONV_HALO:CONV_HALO + tm, :] for cb in range(ch // V7X_LANES)], axis=1)
    mu = jnp.mean(y, axis=-1, keepdims=True)
    yc = y - mu
    var = jnp.mean(yc * yc, axis=-1, keepdims=True)
    z = yc * lax.rsqrt(var + LN_EPS) * g_ref[...] + beta_ref[...]
    o_ref[...] = (z * jax.nn.sigmoid(z)).astype(BF16)


def _conformer_conv(y, w_dw, b_dw, ln_g, ln_b, *, tm, seq, ctx, n_lat_rows):
    rows, ch = y.shape
    hb = tm // CONV_HALO
    n_halo_blocks = rows // CONV_HALO
    n_slab = ch // V7X_LANES
    n_ext = tm + 2 * CONV_HALO
    pitch = n_ext // V7X_SUBLANES
    assert n_ext % V7X_SUBLANES == 0 and pitch % 8 != 0 and B_KERNEL // 2 < CONV_HALO
    ext_rows = -(-(CONV_HALO + pitch + B_KERNEL // 2 + (V7X_SUBLANES - 1) * pitch) // V7X_SUBLANES) * V7X_SUBLANES
    kern = functools.partial(_conv_kernel, n_lat_tiles=n_lat_rows // tm,
                             lat_tiles_per_seq=seq // tm, ctx_tiles_per_seq=ctx // tm)
    nbytes = 2 * (tm + 2 * CONV_HALO) * ch * 4 + 2 * tm * ch * 2 + (2 * tm + 2 * CONV_HALO) * ch * 4 + 6 * tm * ch * 4
    vec = lambda: pl.BlockSpec((1, ch), lambda i: (0, 0))
    return pl.pallas_call(
        kern,
        out_shape=jax.ShapeDtypeStruct((rows, ch), BF16),
        grid=(rows // tm,),
        in_specs=[pl.BlockSpec((CONV_HALO, ch), lambda i: (jnp.maximum(i * hb - 1, 0), 0)),
                  pl.BlockSpec((tm, ch), lambda i: (i, 0)),
                  pl.BlockSpec((CONV_HALO, ch), lambda i: (jnp.minimum((i + 1) * hb, n_halo_blocks - 1), 0)),
                  pl.BlockSpec((B_KERNEL, ch), lambda i: (0, 0)),
                  vec(), vec(), vec()],
        out_specs=pl.BlockSpec((tm, ch), lambda i: (i, 0)),
        scratch_shapes=[pltpu.VMEM((n_slab, ext_rows, V7X_LANES), F32), pltpu.VMEM((n_slab, n_ext, V7X_LANES), F32)],
        compiler_params=_params(nbytes),
        name="conformer_conv",
    )(y, y, y, w_dw, b_dw.reshape(1, ch), ln_g.reshape(1, ch), ln_b.reshape(1, ch))


def _proj1_kernel(x_ref, mod_ref, g_ref, w1_ref, gq_ref, gkv_ref, wq_ref, wk_ref, wvt_ref, rope_ref,
                  q_ref, k_ref, vt_ref, *, q_scale):
    mod = mod_ref[0]
    xm = (_rms(x_ref[...], g_ref[1:2]) * (1.0 + mod[4:5]) + mod[3:4]).astype(BF16)
    c1 = jnp.dot(xm, w1_ref[...], preferred_element_type=F32)
    cos, s_next, s_prev = rope_ref[:, 0:128], rope_ref[:, 128:256], rope_ref[:, 256:384]
    cq = _rms(c1[:, :C_Q_LORA], gq_ref[...] * q_scale).astype(BF16)
    ckv = _rms(c1[:, C_Q_LORA:C_Q_LORA + C_KV_LORA], gkv_ref[...]).astype(BF16)
    kr = _rope128(c1[:, C_Q_LORA + C_KV_LORA:], cos, s_next, s_prev).astype(BF16)
    q = jnp.dot(cq, wq_ref[...], preferred_element_type=F32)
    kn = jnp.dot(ckv, wk_ref[...], preferred_element_type=F32)
    hp = C_HEAD_PAD
    for h in range(C_HEADS):
        q_ref[:, h * hp:h * hp + C_NOPE] = q[:, h * hp:h * hp + C_NOPE].astype(BF16)
        q_ref[:, h * hp + C_NOPE:(h + 1) * hp] = _rope128(
            q[:, h * hp + C_NOPE:(h + 1) * hp], cos, s_next, s_prev).astype(BF16)
        k_ref[:, h * hp:h * hp + C_NOPE] = kn[:, h * C_NOPE:(h + 1) * C_NOPE].astype(BF16)
        k_ref[:, h * hp + C_NOPE:(h + 1) * hp] = kr
    vt_ref[...] = lax.dot_general(wvt_ref[...], ckv, (((1,), (1,)), ((), ())),
                                  preferred_element_type=F32).astype(BF16)


def _proj1(x, mod, g3, w1, gq, gkv, wq, wk, wvt, rope_tab, *, tm, seq, n_lat_rows):
    rows, d = x.shape
    nb = mod.shape[0] - 1
    n_lat_tiles = n_lat_rows // tm
    tiles_per_seq = seq // tm
    rope_idx = lambda i: (jnp.where(i < n_lat_tiles, i % tiles_per_seq, tiles_per_seq), 0)
    qw = C_HEADS * C_HEAD_PAD
    vw = C_HEADS * C_V
    nbytes = (2 * (w1.size + wq.size + wk.size + wvt.size) + 2 * tm * (d * 4 + 384 * 4 + (2 * qw + vw) * 2)
              + tm * (w1.shape[1] + wq.shape[1] + wk.shape[1] + vw) * 4 + tm * d * 6)
    row = lambda w: pl.BlockSpec((tm, w), lambda i: (i, 0))
    return pl.pallas_call(
        functools.partial(_proj1_kernel, q_scale=(C_NOPE + C_ROPE) ** -0.5 * LOG2E),
        out_shape=(jax.ShapeDtypeStruct((rows, qw), BF16),
                   jax.ShapeDtypeStruct((rows, qw), BF16),
                   jax.ShapeDtypeStruct((C_HEADS * C_V, rows), BF16)),
        grid=(rows // tm,),
        in_specs=[row(d),
                  pl.BlockSpec((1, N_MOD, d), lambda i: (jnp.minimum(i * tm // seq, nb), 0, 0)),
                  _resident(g3.shape), _resident(w1.shape), _resident(gq.shape), _resident(gkv.shape),
                  _resident(wq.shape), _resident(wk.shape), _resident(wvt.shape),
                  pl.BlockSpec((tm, 384), rope_idx)],
        out_specs=(row(qw), row(qw), pl.BlockSpec((C_HEADS * C_V, tm), lambda i: (0, i))),
        compiler_params=_params(nbytes),
        name="mla_proj",
    )(x, mod, g3, w1, gq, gkv, wq, wk, wvt, rope_tab)


def _col_reduce(s, op, chains=8):
    r, n = s.shape
    s3 = s.reshape(r // V7X_SUBLANES, V7X_SUBLANES, n)
    chains = min(chains, s3.shape[0])
    step = s3.shape[0] // chains
    parts = []
    for g in range(chains):
        acc = s3[g * step]
        for i in range(g * step + 1, (g + 1) * step if g < chains - 1 else s3.shape[0]):
            acc = op(acc, s3[i])
        parts.append(acc)
    while len(parts) > 1:
        parts = [op(parts[i], parts[i + 1]) for i in range(0, len(parts) - 1, 2)] + (
            [parts[-1]] if len(parts) % 2 else [])
    return parts[0]


def _mla_kernel(q_ref, kl_ref, kc_ref, vtl_ref, vtc_ref, o_ref):
    hp = C_HEAD_PAD
    dn = (((1,), (1,)), ((), ()))

    def scores(h):
        qh = q_ref[:, h * hp:(h + 1) * hp]
        return (lax.dot_general(kl_ref[:, h * hp:(h + 1) * hp], qh, dn, preferred_element_type=F32),
                lax.dot_general(kc_ref[:, h * hp:(h + 1) * hp], qh, dn, preferred_element_type=F32))

    nxt = scores(0)
    for h in range(C_HEADS):
        s1, s2 = nxt
        if h + 1 < C_HEADS:
            nxt = scores(h + 1)
        m = jnp.max(jnp.maximum(_col_reduce(s1, jnp.maximum), _col_reduce(s2, jnp.maximum)),
                    axis=0, keepdims=True)
        e1 = jnp.exp2(s1 - m)
        e2 = jnp.exp2(s2 - m)
        l = jnp.sum(_col_reduce(e1, jnp.add) + _col_reduce(e2, jnp.add), axis=0, keepdims=True)
        ot = (jnp.dot(vtl_ref[h * C_V:(h + 1) * C_V, :], e1.astype(BF16), preferred_element_type=F32)
              + jnp.dot(vtc_ref[h * C_V:(h + 1) * C_V, :], e2.astype(BF16), preferred_element_type=F32))
        o_ref[:, h * C_V:(h + 1) * C_V] = (ot * (1.0 / l)).T.astype(BF16)


def _mla_attention(q, k, vt, *, batch, seq, ctx, tq):
    n_lat = batch * seq
    qw, vw, vpw = k.shape[1], C_HEADS * C_V, vt.shape[0]
    nq = seq // tq
    ctx_blk0 = n_lat // ctx
    nbytes = (2 * (seq + ctx) * (qw + vw) * 2 + 2 * tq * (qw + vw) * 2 + 4 * tq * (seq + ctx) * 4)
    return pl.pallas_call(
        _mla_kernel,
        out_shape=jax.ShapeDtypeStruct((n_lat, vw), BF16),
        grid=(batch, nq),
        in_specs=[pl.BlockSpec((tq, qw), lambda b, i: (b * nq + i, 0)),
                  pl.BlockSpec((seq, qw), lambda b, i: (b, 0)),
                  pl.BlockSpec((ctx, qw), lambda b, i: (ctx_blk0 + b, 0)),
                  pl.BlockSpec((vpw, seq), lambda b, i: (0, b)),
                  pl.BlockSpec((vpw, ctx), lambda b, i: (0, ctx_blk0 + b))],
        out_specs=pl.BlockSpec((tq, vw), lambda b, i: (b * nq + i, 0)),
        compiler_params=_params(nbytes, 2),
        name="mla_attention",
    )(q, k, k, vt, vt)


def _rope_table(seq, pad_rows):
    t = jnp.arange(seq)
    row = (t // GRID_W).astype(F32)
    col = (t % GRID_W).astype(F32)
    d_axis = A_HEAD_DIM // 2
    inv_freq = ROPE_BASE ** (-jnp.arange(0, d_axis, 2, dtype=F32) / d_axis)
    ang_r = row[:, None] * inv_freq
    ang_c = col[:, None] * inv_freq
    ang = jnp.concatenate([ang_r, ang_r, ang_c, ang_c], axis=-1)
    cos, sin = jnp.cos(ang), jnp.sin(ang)
    first_half = (jnp.arange(A_HEAD_DIM) % 32) < 16
    s_next = jnp.where(first_half, -sin, 0.0)
    s_prev = jnp.where(first_half, 0.0, sin)
    tab = jnp.concatenate([jnp.tile(cos, (1, 2)), jnp.tile(s_next, (1, 2)), jnp.tile(s_prev, (1, 2))], axis=-1)
    ident = jnp.concatenate([jnp.ones((pad_rows, 128), F32), jnp.zeros((pad_rows, 256), F32)], axis=-1)
    return jnp.concatenate([tab, ident], axis=0)


def kernel(x, c, ctx, c_ctx, w_mod, b_mod, g_norm, ffn_w_in, ffn_w_out, ab_w_in, a_sink, b_w_dw, b_b_dw,
           b_ln_g, b_ln_b, ab_w_out, c_w_dq, c_g_q, c_w_uq, c_w_dkv, c_g_kv, c_w_uk, c_w_uv, c_w_o, g_final):
    batch, seq, d = x.shape
    ctx_len = ctx.shape[1]
    depth = w_mod.shape[0]
    d_ff = ffn_w_out.shape[2]
    n_lat = batch * seq
    rows = n_lat + batch * ctx_len
    tm = min(512, seq)
    tm_conv = min(256, ctx_len)
    tq = min(2 * V7X_MXU_DIM, seq)
    ck = V7X_MXU_DIM
    assert seq % tm == 0 and (batch * ctx_len) % tm == 0 and seq % (2 * A_BLOCK) == 0 and seq % GRID_W == 0
    assert ctx_len % tm_conv == 0 and seq % tm_conv == 0 and n_lat % ctx_len == 0 and d_ff % ck == 0
    assert depth % 2 == 0 and depth <= 2

    r_mod = -(-(batch + 1) // 16) * 16
    cc = jnp.concatenate([c, c_ctx[None, :], jnp.zeros((r_mod - batch - 1, d), F32)], axis=0)
    mod_all = _modulation(cc, w_mod, b_mod)[:, :batch + 1].reshape(depth, batch + 1, N_MOD, d)

    rope_tab = _rope_table(seq, tm)
    lane_masks = (jnp.arange(128)[None, :] // 64 == jnp.arange(2)[:, None]).astype(BF16)
    t = (x.reshape(n_lat, d), ctx.reshape(batch * ctx_len, d))
    ffn_win, ffn_wout = ffn_w_in.astype(BF16), ffn_w_out.astype(BF16)

    for i in range(depth):
        last = i == depth - 1
        j = i // 2
        mod = mod_all[i]
        g3 = g_norm[i]
        t = _ffn(t, mod, g3, ffn_win, ffn_wout, (i, 0), rows=rows, tm=tm, seq=seq, n_lat_rows=n_lat,
                 sub=0, ck=ck)
        if i % 2 == 0:
            w = ab_w_in[j]
            iq, ik = A_HEADS * A_HEAD_DIM, A_KV_HEADS * A_HEAD_DIM
            wk, wv = w[:, iq:iq + ik], w[:, iq + ik:iq + 2 * ik]
            swap = lambda m: jnp.concatenate([m[:, A_HEAD_DIM:], m[:, :A_HEAD_DIM]], axis=1)
            w_all = jnp.concatenate([w[:, :iq], wk, swap(wk), wv, swap(wv), w[:, iq + 2 * ik:]], axis=1).astype(BF16)
            q, kv, y = _proj0(t, mod, g3, w_all, rope_tab, tm=tm, seq=seq, n_lat_rows=n_lat)
            a = _gqa(q, kv, a_sink[j].astype(F32), lane_masks, batch=batch, seq=seq, ctx=ctx_len)
            bx = _conformer_conv(y, b_w_dw[j], b_b_dw[j], b_ln_g[j], b_ln_b[j],
                                 tm=tm_conv, seq=seq, ctx=ctx_len, n_lat_rows=n_lat)
            wo = ab_w_out[j].astype(BF16)
            attn, wos = (a, bx), (wo[:iq], wo[iq:])
        else:
            hp = C_HEAD_PAD
            wdkv = c_w_dkv[j]
            w1 = jnp.concatenate([c_w_dq[j], wdkv, jnp.zeros((d, 128 - C_ROPE), F32)], axis=1).astype(BF16)
            wuq = c_w_uq[j].reshape(C_Q_LORA, C_HEADS, C_NOPE + C_ROPE)
            wuq = jnp.pad(wuq, ((0, 0), (0, 0), (0, hp - C_NOPE - C_ROPE))).reshape(C_Q_LORA, C_HEADS * hp)
            q, k, vt = _proj1(t, mod, g3, w1, c_g_q[j].reshape(1, -1), c_g_kv[j].reshape(1, -1),
                              wuq.astype(BF16), c_w_uk[j].astype(BF16), c_w_uv[j].T.astype(BF16), rope_tab,
                              tm=tm, seq=seq, n_lat_rows=n_lat)
            a = _mla_attention(q, k, vt, batch=batch, seq=seq, ctx=ctx_len, tq=tq)
            attn, wos = (a,), (c_w_o[j].astype(BF16),)
        t = _ffn(t, mod, g3, ffn_win, ffn_wout, (i, 1), rows=n_lat if last else rows, tm=tm, seq=seq,
                 n_lat_rows=n_lat, sub=2, ck=ck, attn=attn, wo=wos, g_final=g_final if last else None)
    return t.reshape(batch, seq, d)
```

```python
import functools
import math

import jax
import jax.numpy as jnp
from jax import lax
from jax.experimental import pallas as pl
from jax.experimental.pallas import tpu as pltpu

F32 = jnp.float32
BF16 = jnp.bfloat16

GRID_W = 64
ROPE_BASE = 10000.0
NORM_EPS = 1e-6
LN_EPS = 1e-5
NEG_INF = -1e30
LOG2E = math.log2(math.e)
N_MOD = 9
FFN_RES = 0.5
A_HEADS = 8
A_KV_HEADS = 2
A_HEAD_DIM = 64
A_WINDOW = 128
A_BLOCK = 128
B_CH = 512
B_KERNEL = 31
C_HEADS = 8
C_Q_LORA = 256
C_KV_LORA = 256
C_NOPE = 128
C_ROPE = 64
C_V = 128

V7X_LANES = 128
V7X_SUBLANES = 8
V7X_MXU_DIM = 256
V7X_VMEM_BYTES = 64 * 1024 * 1024
V7X_VMEM_USABLE = 56 * 1024 * 1024

C_HEAD_PAD = V7X_MXU_DIM
CONV_HALO = 16


def _vmem_limit(nbytes):
    return int(min(V7X_VMEM_USABLE, max(16 * 1024 * 1024, nbytes * 3 // 2)))


def _params(nbytes, ngrid=1):
    return pltpu.CompilerParams(dimension_semantics=("arbitrary",) * ngrid,
                                vmem_limit_bytes=_vmem_limit(nbytes))


def _rms(x, g):
    return x * lax.rsqrt(jnp.mean(x * x, axis=-1, keepdims=True) + NORM_EPS) * g


def _resident(shape, lead=()):
    nd = len(shape) - len(lead)
    return pl.BlockSpec((None,) * len(lead) + tuple(shape[len(lead):]),
                        lambda *_: tuple(lead) + (0,) * nd, pipeline_mode=pl.Buffered(1))


def _mod_kernel(c_ref, w_ref, b_ref, o_ref):
    c = c_ref[...]
    a = (c * jax.nn.sigmoid(c)).astype(BF16)
    o_ref[0] = jnp.dot(a, w_ref[0].astype(BF16), preferred_element_type=F32) + b_ref[0]


def _modulation(cc, w_mod, b_mod):
    depth, d, n = w_mod.shape
    r = cc.shape[0]
    tn = n // N_MOD
    nbytes = 2 * (d * tn * 4) + d * tn * 2 + 4 * r * (d + 2 * tn) * 4
    return pl.pallas_call(
        _mod_kernel,
        out_shape=jax.ShapeDtypeStruct((depth, r, n), F32),
        grid=(depth, n // tn),
        in_specs=[pl.BlockSpec((r, d), lambda l, j: (0, 0)),
                  pl.BlockSpec((1, d, tn), lambda l, j: (l, 0, j)),
                  pl.BlockSpec((1, 1, tn), lambda l, j: (l, 0, j))],
        out_specs=pl.BlockSpec((1, r, tn), lambda l, j: (l, 0, j)),
        compiler_params=_params(nbytes, 2),
        name="adaln_modulation",
    )(cc, w_mod, b_mod.reshape(depth, 1, n))


def _row_specs(op, tm, n_lat_tiles):
    if not isinstance(op, tuple):
        return [pl.BlockSpec((tm, op.shape[1]), lambda i: (i, 0))]
    lat, cx = op
    return [pl.BlockSpec((tm, lat.shape[1]), lambda i: (jnp.minimum(i, n_lat_tiles - 1), 0)),
            pl.BlockSpec((tm, cx.shape[1]), lambda i: (jnp.maximum(i - n_lat_tiles, 0), 0))]


def _row_load(refs, is_lat):
    if len(refs) == 1:
        return refs[0][...]
    return jnp.where(is_lat, refs[0][...], refs[1][...])


def _ffn_kernel(*refs, arity, sub, final_norm, ck, n_lat_tiles):
    refs = list(refs)
    o_ref = refs.pop()
    take = lambda n: [refs.pop(0) for _ in range(n)]
    x_refs = take(arity[0])
    mod_ref, g_ref = take(2)
    a_refs = [take(n) for n in arity[1:]]
    wo_refs = take(len(arity) - 1)
    win_ref, wout_ref = take(2)
    gf_ref = refs.pop(0) if final_norm else None
    is_lat = pl.program_id(0) < n_lat_tiles

    x = _row_load(x_refs, is_lat)
    mod = mod_ref[0]
    if a_refs:
        ox = None
        for ar, wo_ref in zip(a_refs, wo_refs):
            part = jnp.dot(_row_load(ar, is_lat), wo_ref[...], preferred_element_type=F32)
            ox = part if ox is None else ox + part
        x = x + mod[5:6] * ox
    shift, scale, gate = mod[3 * sub:3 * sub + 1], mod[3 * sub + 1:3 * sub + 2], mod[3 * sub + 2:3 * sub + 3]
    h = (_rms(x, g_ref[sub:sub + 1]) * (1.0 + scale) + shift).astype(BF16)
    d_ff = wout_ref.shape[0]
    acc = None
    for c in range(d_ff // ck):
        g = jnp.dot(h, win_ref[:, c * ck:(c + 1) * ck], preferred_element_type=F32)
        u = jnp.dot(h, win_ref[:, d_ff + c * ck:d_ff + (c + 1) * ck], preferred_element_type=F32)
        act = (g * jax.nn.sigmoid(g) * u).astype(BF16)
        part = jnp.dot(act, wout_ref[c * ck:(c + 1) * ck, :], preferred_element_type=F32)
        acc = part if acc is None else acc + part
    y = x + (FFN_RES * gate) * acc
    if final_norm:
        y = _rms(y, gf_ref[...])
    o_ref[...] = y


def _ffn(x, mod, g3, win, wout, widx, *, rows, tm, seq, n_lat_rows, sub, ck, attn=(), wo=(), g_final=None):
    d = win.shape[-2]
    nb = mod.shape[0] - 1
    n_lat_tiles = n_lat_rows // tm
    arity = tuple(2 if isinstance(o, tuple) else 1 for o in (x, *attn))
    flat = lambda o: list(o) if isinstance(o, tuple) else [o]
    in_specs = _row_specs(x, tm, n_lat_tiles)
    in_specs += [pl.BlockSpec((1, N_MOD, d), lambda i: (jnp.minimum(i * tm // seq, nb), 0, 0)),
                 _resident(g3.shape)]
    for a in attn:
        in_specs += _row_specs(a, tm, n_lat_tiles)
    in_specs += [_resident(w.shape) for w in wo]
    in_specs += [_resident(win.shape, widx), _resident(wout.shape, widx)]
    args = [*flat(x), mod, g3, *[m for a in attn for m in flat(a)], *wo, win, wout]
    if g_final is not None:
        in_specs.append(_resident((1, d)))
        args.append(g_final.reshape(1, d))
    weights = 2 * (math.prod(win.shape[-2:]) + math.prod(wout.shape[-2:]) + sum(w.size for w in wo))
    tiles = 2 * tm * (arity[0] * d * 4 + d * 4 + sum(n * flat(a)[0].shape[1] for n, a in zip(arity[1:], attn)) * 2)
    temps = tm * (d * 4 * 3 + d * 2 + 2 * ck * 4 * 2 + ck * 2)
    kern = functools.partial(_ffn_kernel, arity=arity, sub=sub, final_norm=g_final is not None,
                             ck=ck, n_lat_tiles=n_lat_tiles)
    return pl.pallas_call(
        kern,
        out_shape=jax.ShapeDtypeStruct((rows, d), F32),
        grid=(rows // tm,),
        in_specs=in_specs,
        out_specs=pl.BlockSpec((tm, d), lambda i: (i, 0)),
        compiler_params=_params(weights + tiles + temps),
        name=f"ffn_sub{sub}" + ("_mix" if attn else "") + ("_final" if g_final is not None else ""),
    )(*args)


def _rope128(v, cos, sin_next, sin_prev):
    return (v * cos + pltpu.roll(v, V7X_LANES - 16, 1) * sin_next + pltpu.roll(v, 16, 1) * sin_prev)


def _proj0_kernel(x_ref, mod_ref, g_ref, w_ref, rope_ref, q_ref, kv_ref, y_ref, *, q_scale):
    mod = mod_ref[0]
    xm = (_rms(x_ref[...], g_ref[1:2]) * (1.0 + mod[4:5]) + mod[3:4]).astype(BF16)
    p = jnp.dot(xm, w_ref[...], preferred_element_type=F32)
    cos, s_next, s_prev = rope_ref[:, 0:128], rope_ref[:, 128:256], rope_ref[:, 256:384]
    nq = A_HEADS * A_HEAD_DIM // V7X_LANES
    for j in range(nq):
        blk = p[:, j * 128:(j + 1) * 128]
        q_ref[:, j * 128:(j + 1) * 128] = (_rope128(blk, cos, s_next, s_prev) * q_scale).astype(BF16)
    base = nq * 128
    for j in range(2):
        blk = p[:, base + j * 128:base + (j + 1) * 128]
        kv_ref[:, j * 128:(j + 1) * 128] = _rope128(blk, cos, s_next, s_prev).astype(BF16)
    kv_ref[:, 256:512] = p[:, base + 256:base + 512].astype(BF16)
    a = p[:, base + 512:base + 512 + B_CH]
    g = p[:, base + 512 + B_CH:base + 512 + 2 * B_CH]
    y_ref[...] = a * jax.nn.sigmoid(g)


def _proj0(x, mod, g3, w, rope_tab, *, tm, seq, n_lat_rows):
    rows, d = x.shape
    nb = mod.shape[0] - 1
    n_lat_tiles = n_lat_rows // tm
    tiles_per_seq = seq // tm
    rope_idx = lambda i: (jnp.where(i < n_lat_tiles, i % tiles_per_seq, tiles_per_seq), 0)
    nw = w.shape[1]
    nbytes = 2 * w.size + 2 * tm * (d * 4 + 384 * 4 + 1024 * 2 + 512 * 4) + tm * (nw * 4 + d * 6)
    return pl.pallas_call(
        functools.partial(_proj0_kernel, q_scale=A_HEAD_DIM ** -0.5 * LOG2E),
        out_shape=(jax.ShapeDtypeStruct((rows, 512), BF16),
                   jax.ShapeDtypeStruct((rows, 512), BF16),
                   jax.ShapeDtypeStruct((rows, B_CH), F32)),
        grid=(rows // tm,),
        in_specs=[pl.BlockSpec((tm, d), lambda i: (i, 0)),
                  pl.BlockSpec((1, N_MOD, d), lambda i: (jnp.minimum(i * tm // seq, nb), 0, 0)),
                  _resident(g3.shape), _resident(w.shape),
                  pl.BlockSpec((tm, 384), rope_idx)],
        out_specs=(pl.BlockSpec((tm, 512), lambda i: (i, 0)),
                   pl.BlockSpec((tm, 512), lambda i: (i, 0)),
                   pl.BlockSpec((tm, B_CH), lambda i: (i, 0))),
        compiler_params=_params(nbytes),
        name="ab_in_proj",
    )(x, mod, g3, w, rope_tab)


def _gqa_core(q_ref, kv_blocks, groups, lane_ref, sink_ref, o_ref):
    m_lo, m_hi = lane_ref[0:1, :], lane_ref[1:2, :]
    f_lo, f_hi = m_lo.astype(F32), m_hi.astype(F32)
    k_var, v_var = [], []
    for blk in kv_blocks:
        k, k_sw, v, v_sw = blk[:, 0:128], blk[:, 128:256], blk[:, 256:384], blk[:, 384:512]
        k_var.append(((k * m_lo, k_sw * m_hi), (k_sw * m_lo, k * m_hi)))
        v_var.append(((v * m_lo, v_sw * m_hi), (v_sw * m_lo, v * m_hi)))
    chains = [(g, hk) for g in groups for hk in range(A_KV_HEADS)]

    def scores(group, hk):
        row0, r, ids, _ = group
        qs = jnp.concatenate([q_ref[row0:row0 + r, (2 * hk) * 128:(2 * hk + 1) * 128],
                              q_ref[row0:row0 + r, (2 * hk + 1) * 128:(2 * hk + 2) * 128]], axis=0)
        kcat = jnp.concatenate([k_var[b][hk][0] for b in ids] + [k_var[b][hk][1] for b in ids], axis=0)
        return lax.dot_general(qs, kcat, (((1,), (1,)), ((), ())), preferred_element_type=F32)

    all_scores = [scores(g, hk) for g, hk in chains]
    for (group, hk), s in zip(chains, all_scores):
        row0, r, ids, mask = group
        nk = s.shape[1] // 2
        vcat = jnp.concatenate([v_var[b][hk][0] for b in ids] + [v_var[b][hk][1] for b in ids], axis=0)
        ps, rinv = [], []
        for j in range(2):
            pj, rj = [], []
            for par in range(2):
                sj = s[j * r:(j + 1) * r, par * nk:(par + 1) * nk]
                if mask is not None:
                    sj = jnp.where(mask, sj, NEG_INF)
                sink = sink_ref[4 * hk + 2 * j + par] * LOG2E
                m = jnp.maximum(jnp.max(sj, axis=-1, keepdims=True), sink)
                e = jnp.exp2(sj - m)
                l = jnp.sum(e, axis=-1, keepdims=True) + jnp.exp2(sink - m)
                pj.append(e.astype(BF16))
                rj.append(1.0 / l)
            ps.append(jnp.concatenate(pj, axis=1))
            rinv.append(rj[0] * f_lo + rj[1] * f_hi)
        p = jnp.concatenate(ps, axis=0)
        o = jnp.dot(p, vcat, preferred_element_type=F32)
        for j in range(2):
            blk = 2 * hk + j
            o_ref[row0:row0 + r, blk * 128:(blk + 1) * 128] = (o[j * r:(j + 1) * r] * rinv[j]).astype(BF16)


def _gqa_window_kernel(sink_ref, q_ref, kvp_ref, kvc_ref, kvn_ref, kvx_ref, lane_ref, o_ref, *, npair):
    n = pl.program_id(1)
    ctx = kvx_ref.shape[0]
    nk = 3 * A_BLOCK + ctx
    i = lax.broadcasted_iota(jnp.int32, (A_BLOCK, nk), 0)
    j = lax.broadcasted_iota(jnp.int32, (A_BLOCK, nk), 1)
    band = (j >= i) & (j <= i + 2 * A_WINDOW)
    is_ctx = j >= 3 * A_BLOCK
    lo = jnp.where(n > 0, 0, A_BLOCK)
    hi = jnp.where(n < npair - 1, 3 * A_BLOCK, 2 * A_BLOCK)
    mask0 = (band & (j >= lo)) | is_ctx
    mask1 = (band & (j < hi)) | is_ctx
    cur = kvc_ref[...]
    blocks = [kvp_ref[...], cur[:A_BLOCK], cur[A_BLOCK:], kvn_ref[...], kvx_ref[...]]
    groups = [(0, A_BLOCK, (0, 1, 2, 4), mask0), (A_BLOCK, A_BLOCK, (1, 2, 3, 4), mask1)]
    _gqa_core(q_ref, blocks, groups, lane_ref, sink_ref, o_ref)


def _gqa_ctx_kernel(sink_ref, q_ref, kvx_ref, lane_ref, o_ref):
    _gqa_core(q_ref, [kvx_ref[...]], [(0, q_ref.shape[0], (0,), None)], lane_ref, sink_ref, o_ref)


def _gqa(q, kv, sink, lane_masks, *, batch, seq, ctx):
    rows = q.shape[0]
    nblk = seq // A_BLOCK
    n_lat = batch * seq
    ctx_blk0 = n_lat // ctx
    smem = pl.BlockSpec(memory_space=pltpu.SMEM)
    lane_spec = lambda nd: pl.BlockSpec((2, 128), lambda *_: (0, 0))
    npair = nblk // 2
    one_blk = lambda f: pl.BlockSpec((A_BLOCK, 512), f)
    two_blk = lambda f: pl.BlockSpec((2 * A_BLOCK, 512), f)
    nbytes = 2 * (6 * A_BLOCK + ctx) * 512 * 2 + 10 * (2 * A_BLOCK) * 2 * (3 * A_BLOCK + ctx) * 4
    o_lat = pl.pallas_call(
        functools.partial(_gqa_window_kernel, npair=npair),
        out_shape=jax.ShapeDtypeStruct((n_lat, 512), BF16),
        grid=(batch, npair),
        in_specs=[smem,
                  two_blk(lambda b, n: (b * npair + n, 0)),
                  one_blk(lambda b, n: (b * nblk + jnp.maximum(2 * n - 1, 0), 0)),
                  two_blk(lambda b, n: (b * npair + n, 0)),
                  one_blk(lambda b, n: (b * nblk + jnp.minimum(2 * n + 2, nblk - 1), 0)),
                  pl.BlockSpec((ctx, 512), lambda b, n: (ctx_blk0 + b, 0)),
                  lane_spec(2)],
        out_specs=two_blk(lambda b, n: (b * npair + n, 0)),
        compiler_params=_params(nbytes, 2),
        name="gqa_window",
    )(sink, q, kv, kv, kv, kv, lane_masks)
    nbytes_c = 2 * 3 * ctx * 512 * 2 + 6 * (2 * ctx) * (2 * ctx) * 4
    o_ctx = pl.pallas_call(
        _gqa_ctx_kernel,
        out_shape=jax.ShapeDtypeStruct((rows - n_lat, 512), BF16),
        grid=(batch,),
        in_specs=[smem,
                  pl.BlockSpec((ctx, 512), lambda b: (ctx_blk0 + b, 0)),
                  pl.BlockSpec((ctx, 512), lambda b: (ctx_blk0 + b, 0)),
                  lane_spec(1)],
        out_specs=pl.BlockSpec((ctx, 512), lambda b: (b, 0)),
        compiler_params=_params(nbytes_c, 1),
        name="gqa_context",
    )(sink, q, kv, lane_masks)
    return o_lat, o_ctx


def _conv_kernel(yp_ref, yc_ref, yn_ref, w_ref, b_ref, g_ref, beta_ref, o_ref, ext_ref, acc_ref, *,
                 n_lat_tiles, lat_tiles_per_seq, ctx_tiles_per_seq):
    i = pl.program_id(0)
    tm, ch = yc_ref.shape
    is_lat = i < n_lat_tiles
    pos = jnp.where(is_lat, i % lat_tiles_per_seq, (i - n_lat_tiles) % ctx_tiles_per_seq)
    last_pos = jnp.where(is_lat, lat_tiles_per_seq - 1, ctx_tiles_per_seq - 1)
    keep_prev = (pos > 0).astype(F32)
    keep_next = (pos < last_pos).astype(F32)
    n_ext = tm + 2 * CONV_HALO
    pitch = n_ext // V7X_SUBLANES
    half = B_KERNEL // 2
    pad = CONV_HALO
    zero = jnp.zeros((pad, V7X_LANES), F32)
    for cb in range(ch // V7X_LANES):
        cs = slice(cb * V7X_LANES, (cb + 1) * V7X_LANES)
        ext_ref[cb, 0:pad, :] = zero
        ext_ref[cb, pad:pad + CONV_HALO, :] = yp_ref[:, cs] * keep_prev
        ext_ref[cb, pad + CONV_HALO:pad + CONV_HALO + tm, :] = yc_ref[:, cs]
        ext_ref[cb, pad + CONV_HALO + tm:pad + n_ext, :] = yn_ref[:, cs] * keep_next
        ext_ref[cb, pad + n_ext:, :] = jnp.zeros((ext_ref.shape[1] - pad - n_ext, V7X_LANES), F32)
    for cb in range(ch // V7X_LANES):
        cs = slice(cb * V7X_LANES, (cb + 1) * V7X_LANES)
        for k in range(pitch):
            acc = None
            for t in range(B_KERNEL):
                term = w_ref[t:t + 1, cs] * ext_ref[cb, pl.ds(pad + k + t - half, V7X_SUBLANES, stride=pitch), :]
                acc = term if acc is None else acc + term
            acc_ref[cb, pl.ds(k, V7X_SUBLANES, stride=pitch), :] = acc + b_ref[:, cs]
    y = jnp.concatenate([acc_ref[cb, CONV_HALO:CONV_HALO + tm, :] for cb in range(ch // V7X_LANES)], axis=1)
    mu = jnp.mean(y, axis=-1, keepdims=True)
    yc = y - mu
    var = jnp.mean(yc * yc, axis=-1, keepdims=True)
    z = yc * lax.rsqrt(var + LN_EPS) * g_ref[...] + beta_ref[...]
    o_ref[...] = (z * jax.nn.sigmoid(z)).astype(BF16)


def _conformer_conv(y, w_dw, b_dw, ln_g, ln_b, *, tm, seq, ctx, n_lat_rows):
    rows, ch = y.shape
    hb = tm // CONV_HALO
    n_halo_blocks = rows // CONV_HALO
    n_slab = ch // V7X_LANES
    n_ext = tm + 2 * CONV_HALO
    pitch = n_ext // V7X_SUBLANES
    assert n_ext % V7X_SUBLANES == 0 and pitch % 8 != 0 and B_KERNEL // 2 < CONV_HALO
    ext_rows = -(-(CONV_HALO + pitch + B_KERNEL // 2 + (V7X_SUBLANES - 1) * pitch) // V7X_SUBLANES) * V7X_SUBLANES
    kern = functools.partial(_conv_kernel, n_lat_tiles=n_lat_rows // tm,
                             lat_tiles_per_seq=seq // tm, ctx_tiles_per_seq=ctx // tm)
    nbytes = 2 * (tm + 2 * CONV_HALO) * ch * 4 + 2 * tm * ch * 2 + (2 * tm + 2 * CONV_HALO) * ch * 4 + 6 * tm * ch * 4
    vec = lambda: pl.BlockSpec((1, ch), lambda i: (0, 0))
    return pl.pallas_call(
        kern,
        out_shape=jax.ShapeDtypeStruct((rows, ch), BF16),
        grid=(rows // tm,),
        in_specs=[pl.BlockSpec((CONV_HALO, ch), lambda i: (jnp.maximum(i * hb - 1, 0), 0)),
                  pl.BlockSpec((tm, ch), lambda i: (i, 0)),
                  pl.BlockSpec((CONV_HALO, ch), lambda i: (jnp.minimum((i + 1) * hb, n_halo_blocks - 1), 0)),
                  pl.BlockSpec((B_KERNEL, ch), lambda i: (0, 0)),
                  vec(), vec(), vec()],
        out_specs=pl.BlockSpec((tm, ch), lambda i: (i, 0)),
        scratch_shapes=[pltpu.VMEM((n_slab, ext_rows, V7X_LANES), F32), pltpu.VMEM((n_slab, n_ext, V7X_LANES), F32)],
        compiler_params=_params(nbytes),
        name="conformer_conv",
    )(y, y, y, w_dw, b_dw.reshape(1, ch), ln_g.reshape(1, ch), ln_b.reshape(1, ch))


def _proj1_kernel(x_ref, mod_ref, g_ref, w1_ref, gq_ref, gkv_ref, wq_ref, wk_ref, wvt_ref, rope_ref,
                  q_ref, k_ref, vt_ref, *, q_scale):
    mod = mod_ref[0]
    xm = (_rms(x_ref[...], g_ref[1:2]) * (1.0 + mod[4:5]) + mod[3:4]).astype(BF16)
    c1 = jnp.dot(xm, w1_ref[...], preferred_element_type=F32)
    cos, s_next, s_prev = rope_ref[:, 0:128], rope_ref[:, 128:256], rope_ref[:, 256:384]
    cq = _rms(c1[:, :C_Q_LORA], gq_ref[...] * q_scale).astype(BF16)
    ckv = _rms(c1[:, C_Q_LORA:C_Q_LORA + C_KV_LORA], gkv_ref[...]).astype(BF16)
    kr = _rope128(c1[:, C_Q_LORA + C_KV_LORA:], cos, s_next, s_prev).astype(BF16)
    q = jnp.dot(cq, wq_ref[...], preferred_element_type=F32)
    kn = jnp.dot(ckv, wk_ref[...], preferred_element_type=F32)
    hp = C_HEAD_PAD
    for h in range(C_HEADS):
        q_ref[:, h * hp:h * hp + C_NOPE] = q[:, h * hp:h * hp + C_NOPE].astype(BF16)
        q_ref[:, h * hp + C_NOPE:(h + 1) * hp] = _rope128(
            q[:, h * hp + C_NOPE:(h + 1) * hp], cos, s_next, s_prev).astype(BF16)
        k_ref[:, h * hp:h * hp + C_NOPE] = kn[:, h * C_NOPE:(h + 1) * C_NOPE].astype(BF16)
        k_ref[:, h * hp + C_NOPE:(h + 1) * hp] = kr
    vt_ref[0] = lax.dot_general(wvt_ref[...], ckv, (((1,), (1,)), ((), ())),
                                preferred_element_type=F32).astype(BF16)


def _proj1(x, mod, g3, w1, gq, gkv, wq, wk, wvt, rope_tab, *, tm, seq, n_lat_rows):
    rows, d = x.shape
    nb = mod.shape[0] - 1
    n_lat_tiles = n_lat_rows // tm
    tiles_per_seq = seq // tm
    rope_idx = lambda i: (jnp.where(i < n_lat_tiles, i % tiles_per_seq, tiles_per_seq), 0)
    qw = C_HEADS * C_HEAD_PAD
    vw = C_HEADS * C_V
    nbytes = (2 * (w1.size + wq.size + wk.size + wvt.size) + 2 * tm * (d * 4 + 384 * 4 + (2 * qw + vw) * 2)
              + tm * (w1.shape[1] + wq.shape[1] + wk.shape[1] + vw) * 4 + tm * d * 6)
    row = lambda w: pl.BlockSpec((tm, w), lambda i: (i, 0))
    return pl.pallas_call(
        functools.partial(_proj1_kernel, q_scale=(C_NOPE + C_ROPE) ** -0.5 * LOG2E),
        out_shape=(jax.ShapeDtypeStruct((rows, qw), BF16),
                   jax.ShapeDtypeStruct((rows, qw), BF16),
                   jax.ShapeDtypeStruct((rows // tm, C_HEADS * C_V, tm), BF16)),
        grid=(rows // tm,),
        in_specs=[row(d),
                  pl.BlockSpec((1, N_MOD, d), lambda i: (jnp.minimum(i * tm // seq, nb), 0, 0)),
                  _resident(g3.shape), _resident(w1.shape), _resident(gq.shape), _resident(gkv.shape),
                  _resident(wq.shape), _resident(wk.shape), _resident(wvt.shape),
                  pl.BlockSpec((tm, 384), rope_idx)],
        out_specs=(row(qw), row(qw), pl.BlockSpec((1, C_HEADS * C_V, tm), lambda i: (i, 0, 0))),
        compiler_params=_params(nbytes),
        name="mla_proj",
    )(x, mod, g3, w1, gq, gkv, wq, wk, wvt, rope_tab)


def _col_reduce(s, op, chains=8):
    r, n = s.shape
    s3 = s.reshape(r // V7X_SUBLANES, V7X_SUBLANES, n)
    chains = min(chains, s3.shape[0])
    step = s3.shape[0] // chains
    parts = []
    for g in range(chains):
        acc = s3[g * step]
        for i in range(g * step + 1, (g + 1) * step if g < chains - 1 else s3.shape[0]):
            acc = op(acc, s3[i])
        parts.append(acc)
    while len(parts) > 1:
        parts = [op(parts[i], parts[i + 1]) for i in range(0, len(parts) - 1, 2)] + (
            [parts[-1]] if len(parts) % 2 else [])
    return parts[0]


def _mla_kernel(q_ref, kl_ref, kc_ref, vtl_ref, vtc_ref, o_ref):
    hp = C_HEAD_PAD
    dn = (((1,), (1,)), ((), ()))

    def scores(h):
        qh = q_ref[:, h * hp:(h + 1) * hp]
        return (lax.dot_general(kl_ref[:, h * hp:(h + 1) * hp], qh, dn, preferred_element_type=F32),
                lax.dot_general(kc_ref[:, h * hp:(h + 1) * hp], qh, dn, preferred_element_type=F32))

    nxt = scores(0)
    for h in range(C_HEADS):
        s1, s2 = nxt
        if h + 1 < C_HEADS:
            nxt = scores(h + 1)
        m = jnp.max(jnp.maximum(_col_reduce(s1, jnp.maximum), _col_reduce(s2, jnp.maximum)),
                    axis=0, keepdims=True)
        e1 = jnp.exp2(s1 - m)
        e2 = jnp.exp2(s2 - m)
        l = jnp.sum(_col_reduce(e1, jnp.add) + _col_reduce(e2, jnp.add), axis=0, keepdims=True)
        e1b = e1.astype(BF16)
        tk = vtl_ref.shape[2]
        ot = jnp.dot(vtc_ref[0, h * C_V:(h + 1) * C_V, :], e2.astype(BF16), preferred_element_type=F32)
        for j in range(vtl_ref.shape[0]):
            ot = ot + jnp.dot(vtl_ref[j, h * C_V:(h + 1) * C_V, :], e1b[j * tk:(j + 1) * tk],
                              preferred_element_type=F32)
        o_ref[:, h * C_V:(h + 1) * C_V] = (ot * (1.0 / l)).T.astype(BF16)


def _mla_attention(q, k, vt, *, batch, seq, ctx, tq):
    n_lat = batch * seq
    qw, vw = k.shape[1], C_HEADS * C_V
    tk = vt.shape[2]
    assert seq % tk == 0 and tk % ctx == 0
    nq = seq // tq
    ctx_blk0 = n_lat // ctx
    nbytes = (2 * (seq + ctx) * (qw + vw) * 2 + 2 * tq * (qw + vw) * 2 + 4 * tq * (seq + ctx) * 4)
    return pl.pallas_call(
        _mla_kernel,
        out_shape=jax.ShapeDtypeStruct((n_lat, vw), BF16),
        grid=(batch, nq),
        in_specs=[pl.BlockSpec((tq, qw), lambda b, i: (b * nq + i, 0)),
                  pl.BlockSpec((seq, qw), lambda b, i: (b, 0)),
                  pl.BlockSpec((ctx, qw), lambda b, i: (ctx_blk0 + b, 0)),
                  pl.BlockSpec((seq // tk, vw, tk), lambda b, i: (b, 0, 0)),
                  pl.BlockSpec((1, vw, ctx), lambda b, i: ((n_lat + b * ctx) // tk, 0, (b * ctx % tk) // ctx))],
        out_specs=pl.BlockSpec((tq, vw), lambda b, i: (b * nq + i, 0)),
        compiler_params=_params(nbytes, 2),
        name="mla_attention",
    )(q, k, k, vt, vt)


def _rope_table(seq, pad_rows):
    t = jnp.arange(seq)
    row = (t // GRID_W).astype(F32)
    col = (t % GRID_W).astype(F32)
    d_axis = A_HEAD_DIM // 2
    inv_freq = ROPE_BASE ** (-jnp.arange(0, d_axis, 2, dtype=F32) / d_axis)
    ang_r = row[:, None] * inv_freq
    ang_c = col[:, None] * inv_freq
    ang = jnp.concatenate([ang_r, ang_r, ang_c, ang_c], axis=-1)
    cos, sin = jnp.cos(ang), jnp.sin(ang)
    first_half = (jnp.arange(A_HEAD_DIM) % 32) < 16
    s_next = jnp.where(first_half, -sin, 0.0)
    s_prev = jnp.where(first_half, 0.0, sin)
    tab = jnp.concatenate([jnp.tile(cos, (1, 2)), jnp.tile(s_next, (1, 2)), jnp.tile(s_prev, (1, 2))], axis=-1)
    ident = jnp.concatenate([jnp.ones((pad_rows, 128), F32), jnp.zeros((pad_rows, 256), F32)], axis=-1)
    return jnp.concatenate([tab, ident], axis=0)


def kernel(x, c, ctx, c_ctx, w_mod, b_mod, g_norm, ffn_w_in, ffn_w_out, ab_w_in, a_sink, b_w_dw, b_b_dw,
           b_ln_g, b_ln_b, ab_w_out, c_w_dq, c_g_q, c_w_uq, c_w_dkv, c_g_kv, c_w_uk, c_w_uv, c_w_o, g_final):
    batch, seq, d = x.shape
    ctx_len = ctx.shape[1]
    depth = w_mod.shape[0]
    d_ff = ffn_w_out.shape[2]
    n_lat = batch * seq
    rows = n_lat + batch * ctx_len
    tm = min(512, seq)
    tm_conv = min(256, ctx_len)
    tq = min(2 * V7X_MXU_DIM, seq)
    ck = V7X_MXU_DIM
    assert seq % tm == 0 and (batch * ctx_len) % tm == 0 and seq % (2 * A_BLOCK) == 0 and seq % GRID_W == 0
    assert ctx_len % tm_conv == 0 and seq % tm_conv == 0 and n_lat % ctx_len == 0 and d_ff % ck == 0
    assert depth % 2 == 0 and depth <= 2

    r_mod = -(-(batch + 1) // 16) * 16
    cc = jnp.concatenate([c, c_ctx[None, :], jnp.zeros((r_mod - batch - 1, d), F32)], axis=0)
    mod_all = _modulation(cc, w_mod, b_mod)[:, :batch + 1].reshape(depth, batch + 1, N_MOD, d)

    rope_tab = _rope_table(seq, tm)
    lane_masks = (jnp.arange(128)[None, :] // 64 == jnp.arange(2)[:, None]).astype(BF16)
    t = (x.reshape(n_lat, d), ctx.reshape(batch * ctx_len, d))
    ffn_win, ffn_wout = ffn_w_in.astype(BF16), ffn_w_out.astype(BF16)

    for i in range(depth):
        last = i == depth - 1
        j = i // 2
        mod = mod_all[i]
        g3 = g_norm[i]
        t = _ffn(t, mod, g3, ffn_win, ffn_wout, (i, 0), rows=rows, tm=tm, seq=seq, n_lat_rows=n_lat,
                 sub=0, ck=ck)
        if i % 2 == 0:
            w = ab_w_in[j]
            iq, ik = A_HEADS * A_HEAD_DIM, A_KV_HEADS * A_HEAD_DIM
            wk, wv = w[:, iq:iq + ik], w[:, iq + ik:iq + 2 * ik]
            swap = lambda m: jnp.concatenate([m[:, A_HEAD_DIM:], m[:, :A_HEAD_DIM]], axis=1)
            w_all = jnp.concatenate([w[:, :iq], wk, swap(wk), wv, swap(wv), w[:, iq + 2 * ik:]], axis=1).astype(BF16)
            q, kv, y = _proj0(t, mod, g3, w_all, rope_tab, tm=tm, seq=seq, n_lat_rows=n_lat)
            a = _gqa(q, kv, a_sink[j].astype(F32), lane_masks, batch=batch, seq=seq, ctx=ctx_len)
            bx = _conformer_conv(y, b_w_dw[j], b_b_dw[j], b_ln_g[j], b_ln_b[j],
                                 tm=tm_conv, seq=seq, ctx=ctx_len, n_lat_rows=n_lat)
            wo = ab_w_out[j].astype(BF16)
            attn, wos = (a, bx), (wo[:iq], wo[iq:])
        else:
            hp = C_HEAD_PAD
            wdkv = c_w_dkv[j]
            w1 = jnp.concatenate([c_w_dq[j], wdkv, jnp.zeros((d, 128 - C_ROPE), F32)], axis=1).astype(BF16)
            wuq = c_w_uq[j].reshape(C_Q_LORA, C_HEADS, C_NOPE + C_ROPE)
            wuq = jnp.pad(wuq, ((0, 0), (0, 0), (0, hp - C_NOPE - C_ROPE))).reshape(C_Q_LORA, C_HEADS * hp)
            q, k, vt = _proj1(t, mod, g3, w1, c_g_q[j].reshape(1, -1), c_g_kv[j].reshape(1, -1),
                              wuq.astype(BF16), c_w_uk[j].astype(BF16), c_w_uv[j].T.astype(BF16), rope_tab,
                              tm=tm, seq=seq, n_lat_rows=n_lat)
            a = _mla_attention(q, k, vt, batch=batch, seq=seq, ctx=ctx_len, tq=tq)
            attn, wos = (a,), (c_w_o[j].astype(BF16),)
        t = _ffn(t, mod, g3, ffn_win, ffn_wout, (i, 1), rows=n_lat if last else rows, tm=tm, seq=seq,
                 n_lat_rows=n_lat, sub=2, ck=ck, attn=attn, wo=wos, g_final=g_final if last else None)
    return t.reshape(batch, seq, d)
```

```python
import functools
import math

import jax
import jax.numpy as jnp
from jax import lax
from jax.experimental import pallas as pl
from jax.experimental.pallas import tpu as pltpu

F32 = jnp.float32
BF16 = jnp.bfloat16

GRID_W = 64
ROPE_BASE = 10000.0
NORM_EPS = 1e-6
LN_EPS = 1e-5
NEG_INF = -1e30
LOG2E = math.log2(math.e)
N_MOD = 9
FFN_RES = 0.5
A_HEADS = 8
A_KV_HEADS = 2
A_HEAD_DIM = 64
A_WINDOW = 128
A_BLOCK = 128
B_CH = 512
B_KERNEL = 31
C_HEADS = 8
C_Q_LORA = 256
C_KV_LORA = 256
C_NOPE = 128
C_ROPE = 64
C_V = 128

V7X_LANES = 128
V7X_SUBLANES = 8
V7X_MXU_DIM = 256
V7X_VMEM_BYTES = 64 * 1024 * 1024
V7X_VMEM_USABLE = 56 * 1024 * 1024

C_HEAD_PAD = V7X_MXU_DIM
CONV_HALO = 16
GQA_BLOCKS = 8


def _vmem_limit(nbytes):
    return int(min(V7X_VMEM_USABLE, max(16 * 1024 * 1024, nbytes * 3 // 2)))


def _params(nbytes, ngrid=1):
    return pltpu.CompilerParams(dimension_semantics=("arbitrary",) * ngrid,
                                vmem_limit_bytes=_vmem_limit(nbytes))


def _rms(x, g):
    return x * lax.rsqrt(jnp.mean(x * x, axis=-1, keepdims=True) + NORM_EPS) * g


def _resident(shape, lead=()):
    nd = len(shape) - len(lead)
    return pl.BlockSpec((None,) * len(lead) + tuple(shape[len(lead):]),
                        lambda *_: tuple(lead) + (0,) * nd, pipeline_mode=pl.Buffered(1))


def _mod_kernel(c_ref, w_ref, b_ref, o_ref):
    c = c_ref[...]
    a = (c * jax.nn.sigmoid(c)).astype(BF16)
    o_ref[0] = jnp.dot(a, w_ref[0].astype(BF16), preferred_element_type=F32) + b_ref[0]


def _modulation(cc, w_mod, b_mod):
    depth, d, n = w_mod.shape
    r = cc.shape[0]
    tn = n // N_MOD
    nbytes = 2 * (d * tn * 4) + d * tn * 2 + 4 * r * (d + 2 * tn) * 4
    return pl.pallas_call(
        _mod_kernel,
        out_shape=jax.ShapeDtypeStruct((depth, r, n), F32),
        grid=(depth, n // tn),
        in_specs=[pl.BlockSpec((r, d), lambda l, j: (0, 0)),
                  pl.BlockSpec((1, d, tn), lambda l, j: (l, 0, j)),
                  pl.BlockSpec((1, 1, tn), lambda l, j: (l, 0, j))],
        out_specs=pl.BlockSpec((1, r, tn), lambda l, j: (l, 0, j)),
        compiler_params=_params(nbytes, 2),
        name="adaln_modulation",
    )(cc, w_mod, b_mod.reshape(depth, 1, n))


def _row_specs(op, tm, n_lat_tiles):
    if not isinstance(op, tuple):
        return [pl.BlockSpec((tm, op.shape[1]), lambda i: (i, 0))]
    lat, cx = op
    return [pl.BlockSpec((tm, lat.shape[1]), lambda i: (jnp.minimum(i, n_lat_tiles - 1), 0)),
            pl.BlockSpec((tm, cx.shape[1]), lambda i: (jnp.maximum(i - n_lat_tiles, 0), 0))]


def _row_load(refs, is_lat):
    if len(refs) == 1:
        return refs[0][...]
    return jnp.where(is_lat, refs[0][...], refs[1][...])


def _ffn_kernel(*refs, arity, sub, final_norm, ck, n_lat_tiles):
    refs = list(refs)
    o_ref = refs.pop()
    take = lambda n: [refs.pop(0) for _ in range(n)]
    x_refs = take(arity[0])
    mod_ref, g_ref = take(2)
    a_refs = [take(n) for n in arity[1:]]
    wo_refs = take(len(arity) - 1)
    win_ref, wout_ref = take(2)
    gf_ref = refs.pop(0) if final_norm else None
    is_lat = pl.program_id(0) < n_lat_tiles

    x = _row_load(x_refs, is_lat)
    mod = mod_ref[0]
    if a_refs:
        ox = None
        for ar, wo_ref in zip(a_refs, wo_refs):
            part = jnp.dot(_row_load(ar, is_lat), wo_ref[...], preferred_element_type=F32)
            ox = part if ox is None else ox + part
        x = x + mod[5:6] * ox
    shift, scale, gate = mod[3 * sub:3 * sub + 1], mod[3 * sub + 1:3 * sub + 2], mod[3 * sub + 2:3 * sub + 3]
    h = (_rms(x, g_ref[sub:sub + 1]) * (1.0 + scale) + shift).astype(BF16)
    d_ff = wout_ref.shape[0]
    acc = None
    for c in range(d_ff // ck):
        g = jnp.dot(h, win_ref[:, c * ck:(c + 1) * ck], preferred_element_type=F32)
        u = jnp.dot(h, win_ref[:, d_ff + c * ck:d_ff + (c + 1) * ck], preferred_element_type=F32)
        act = (g * jax.nn.sigmoid(g) * u).astype(BF16)
        part = jnp.dot(act, wout_ref[c * ck:(c + 1) * ck, :], preferred_element_type=F32)
        acc = part if acc is None else acc + part
    y = x + (FFN_RES * gate) * acc
    if final_norm:
        y = _rms(y, gf_ref[...])
    o_ref[...] = y


def _ffn(x, mod, g3, win, wout, widx, *, rows, tm, seq, n_lat_rows, sub, ck, attn=(), wo=(), g_final=None):
    d = win.shape[-2]
    nb = mod.shape[0] - 1
    n_lat_tiles = n_lat_rows // tm
    arity = tuple(2 if isinstance(o, tuple) else 1 for o in (x, *attn))
    flat = lambda o: list(o) if isinstance(o, tuple) else [o]
    in_specs = _row_specs(x, tm, n_lat_tiles)
    in_specs += [pl.BlockSpec((1, N_MOD, d), lambda i: (jnp.minimum(i * tm // seq, nb), 0, 0)),
                 _resident(g3.shape)]
    for a in attn:
        in_specs += _row_specs(a, tm, n_lat_tiles)
    in_specs += [_resident(w.shape) for w in wo]
    in_specs += [_resident(win.shape, widx), _resident(wout.shape, widx)]
    args = [*flat(x), mod, g3, *[m for a in attn for m in flat(a)], *wo, win, wout]
    if g_final is not None:
        in_specs.append(_resident((1, d)))
        args.append(g_final.reshape(1, d))
    weights = 2 * (math.prod(win.shape[-2:]) + math.prod(wout.shape[-2:]) + sum(w.size for w in wo))
    tiles = 2 * tm * (arity[0] * d * 4 + d * 4 + sum(n * flat(a)[0].shape[1] for n, a in zip(arity[1:], attn)) * 2)
    temps = tm * (d * 4 * 3 + d * 2 + 2 * ck * 4 * 2 + ck * 2)
    kern = functools.partial(_ffn_kernel, arity=arity, sub=sub, final_norm=g_final is not None,
                             ck=ck, n_lat_tiles=n_lat_tiles)
    return pl.pallas_call(
        kern,
        out_shape=jax.ShapeDtypeStruct((rows, d), F32),
        grid=(rows // tm,),
        in_specs=in_specs,
        out_specs=pl.BlockSpec((tm, d), lambda i: (i, 0)),
        compiler_params=_params(weights + tiles + temps),
        name=f"ffn_sub{sub}" + ("_mix" if attn else "") + ("_final" if g_final is not None else ""),
    )(*args)


def _rope128(v, cos, sin_next, sin_prev):
    return (v * cos + pltpu.roll(v, V7X_LANES - 16, 1) * sin_next + pltpu.roll(v, 16, 1) * sin_prev)


def _proj0_kernel(x_ref, mod_ref, g_ref, w_ref, rope_ref, q_ref, kv_ref, y_ref, *, q_scale):
    mod = mod_ref[0]
    xm = (_rms(x_ref[...], g_ref[1:2]) * (1.0 + mod[4:5]) + mod[3:4]).astype(BF16)
    nq = A_HEADS * A_HEAD_DIM // V7X_LANES
    base = nq * 128
    p_qk = jnp.dot(xm, w_ref[:, :base + 256], preferred_element_type=F32)
    p_ag = jnp.dot(xm, w_ref[:, base + 512:], preferred_element_type=F32)
    cos, s_next, s_prev = rope_ref[:, 0:128], rope_ref[:, 128:256], rope_ref[:, 256:384]
    for j in range(nq):
        blk = p_qk[:, j * 128:(j + 1) * 128]
        q_ref[:, j * 128:(j + 1) * 128] = (_rope128(blk, cos, s_next, s_prev) * q_scale).astype(BF16)
    for j in range(2):
        blk = p_qk[:, base + j * 128:base + (j + 1) * 128]
        kv_ref[:, j * 128:(j + 1) * 128] = _rope128(blk, cos, s_next, s_prev).astype(BF16)
    p_v = jnp.dot(xm, w_ref[:, base + 256:base + 512], preferred_element_type=F32)
    y_ref[...] = p_ag[:, :B_CH] * jax.nn.sigmoid(p_ag[:, B_CH:])
    kv_ref[:, 256:512] = p_v.astype(BF16)


def _proj0(x, mod, g3, w, rope_tab, *, tm, seq, n_lat_rows):
    rows, d = x.shape
    nb = mod.shape[0] - 1
    n_lat_tiles = n_lat_rows // tm
    tiles_per_seq = seq // tm
    rope_idx = lambda i: (jnp.where(i < n_lat_tiles, i % tiles_per_seq, tiles_per_seq), 0)
    nw = w.shape[1]
    nbytes = 2 * w.size + 2 * tm * (d * 4 + 384 * 4 + 1024 * 2 + 512 * 4) + tm * (nw * 4 + d * 6)
    return pl.pallas_call(
        functools.partial(_proj0_kernel, q_scale=A_HEAD_DIM ** -0.5 * LOG2E),
        out_shape=(jax.ShapeDtypeStruct((rows, 512), BF16),
                   jax.ShapeDtypeStruct((rows, 512), BF16),
                   jax.ShapeDtypeStruct((rows, B_CH), F32)),
        grid=(rows // tm,),
        in_specs=[pl.BlockSpec((tm, d), lambda i: (i, 0)),
                  pl.BlockSpec((1, N_MOD, d), lambda i: (jnp.minimum(i * tm // seq, nb), 0, 0)),
                  _resident(g3.shape), _resident(w.shape),
                  pl.BlockSpec((tm, 384), rope_idx)],
        out_specs=(pl.BlockSpec((tm, 512), lambda i: (i, 0)),
                   pl.BlockSpec((tm, 512), lambda i: (i, 0)),
                   pl.BlockSpec((tm, B_CH), lambda i: (i, 0))),
        compiler_params=_params(nbytes),
        name="ab_in_proj",
    )(x, mod, g3, w, rope_tab)


def _gqa_core(q_ref, kv_blocks, groups, lane_ref, sink_ref, o_ref):
    m_lo, m_hi = lane_ref[0:1, :], lane_ref[1:2, :]
    f_lo, f_hi = m_lo.astype(F32), m_hi.astype(F32)
    k_var, v_var = [], []
    for blk in kv_blocks:
        k, k_sw, v, v_sw = blk[:, 0:128], blk[:, 128:256], blk[:, 256:384], blk[:, 384:512]
        k_var.append(((k * m_lo, k_sw * m_hi), (k_sw * m_lo, k * m_hi)))
        v_var.append(((v * m_lo, v_sw * m_hi), (v_sw * m_lo, v * m_hi)))
    chains = [(g, hk) for g in groups for hk in range(A_KV_HEADS)]

    def scores(group, hk):
        row0, r, ids, _ = group
        qs = jnp.concatenate([q_ref[row0:row0 + r, (2 * hk) * 128:(2 * hk + 1) * 128],
                              q_ref[row0:row0 + r, (2 * hk + 1) * 128:(2 * hk + 2) * 128]], axis=0)
        kcat = jnp.concatenate([k_var[b][hk][0] for b in ids] + [k_var[b][hk][1] for b in ids], axis=0)
        return lax.dot_general(qs, kcat, (((1,), (1,)), ((), ())), preferred_element_type=F32)

    all_scores = [scores(g, hk) for g, hk in chains]
    for (group, hk), s in zip(chains, all_scores):
        row0, r, ids, mask = group
        nk = s.shape[1] // 2
        vcat = jnp.concatenate([v_var[b][hk][0] for b in ids] + [v_var[b][hk][1] for b in ids], axis=0)
        ps, rinv = [], []
        for j in range(2):
            pj, rj = [], []
            for par in range(2):
                sj = s[j * r:(j + 1) * r, par * nk:(par + 1) * nk]
                if mask is not None:
                    sj = jnp.where(mask, sj, NEG_INF)
                sink = sink_ref[4 * hk + 2 * j + par] * LOG2E
                m = jnp.maximum(jnp.max(sj, axis=-1, keepdims=True), sink)
                e = jnp.exp2(sj - m)
                l = jnp.sum(e, axis=-1, keepdims=True) + jnp.exp2(sink - m)
                pj.append(e.astype(BF16))
                rj.append(1.0 / l)
            ps.append(jnp.concatenate(pj, axis=1))
            rinv.append(rj[0] * f_lo + rj[1] * f_hi)
        p = jnp.concatenate(ps, axis=0)
        o = jnp.dot(p, vcat, preferred_element_type=F32)
        for j in range(2):
            blk = 2 * hk + j
            o_ref[row0:row0 + r, blk * 128:(blk + 1) * 128] = (o[j * r:(j + 1) * r] * rinv[j]).astype(BF16)


def _gqa_window_kernel(sink_ref, q_ref, kvp_ref, kvc_ref, kvn_ref, kvx_ref, lane_ref, o_ref, *, nstep):
    n = pl.program_id(1)
    nb = q_ref.shape[0] // A_BLOCK
    ctx = kvx_ref.shape[0]
    nk = 3 * A_BLOCK + ctx
    i = lax.broadcasted_iota(jnp.int32, (A_BLOCK, nk), 0)
    j = lax.broadcasted_iota(jnp.int32, (A_BLOCK, nk), 1)
    band = (j >= i) & (j <= i + 2 * A_WINDOW)
    is_ctx = j >= 3 * A_BLOCK
    lo = jnp.where(n > 0, 0, A_BLOCK)
    hi = jnp.where(n < nstep - 1, 3 * A_BLOCK, 2 * A_BLOCK)
    mask_mid = band | is_ctx
    cur = kvc_ref[...]
    blocks = [kvp_ref[...]] + [cur[g * A_BLOCK:(g + 1) * A_BLOCK] for g in range(nb)] + [kvn_ref[...], kvx_ref[...]]
    groups = []
    for g in range(nb):
        mask = mask_mid
        if g == 0:
            mask = (band & (j >= lo)) | is_ctx
        if g == nb - 1:
            mask = (band & (j < hi)) | is_ctx if g else (band & (j >= lo) & (j < hi)) | is_ctx
        groups.append((g * A_BLOCK, A_BLOCK, (g, g + 1, g + 2, nb + 2), mask))
    _gqa_core(q_ref, blocks, groups, lane_ref, sink_ref, o_ref)


def _gqa_ctx_kernel(sink_ref, q_ref, kvx_ref, lane_ref, o_ref):
    _gqa_core(q_ref, [kvx_ref[...]], [(0, q_ref.shape[0], (0,), None)], lane_ref, sink_ref, o_ref)


def _gqa(q, kv, sink, lane_masks, *, batch, seq, ctx):
    rows = q.shape[0]
    nblk = seq // A_BLOCK
    n_lat = batch * seq
    ctx_blk0 = n_lat // ctx
    smem = pl.BlockSpec(memory_space=pltpu.SMEM)
    lane_spec = lambda nd: pl.BlockSpec((2, 128), lambda *_: (0, 0))
    nb = GQA_BLOCKS if nblk % GQA_BLOCKS == 0 else 1
    nstep = nblk // nb
    one_blk = lambda f: pl.BlockSpec((A_BLOCK, 512), f)
    step_blk = lambda f: pl.BlockSpec((nb * A_BLOCK, 512), f)
    nbytes = 2 * ((2 * nb + 2) * A_BLOCK + ctx) * 512 * 2 + 5 * nb * (2 * A_BLOCK) * 2 * (3 * A_BLOCK + ctx) * 4
    o_lat = pl.pallas_call(
        functools.partial(_gqa_window_kernel, nstep=nstep),
        out_shape=jax.ShapeDtypeStruct((n_lat, 512), BF16),
        grid=(batch, nstep),
        in_specs=[smem,
                  step_blk(lambda b, n: (b * nstep + n, 0)),
                  one_blk(lambda b, n: (b * nblk + jnp.maximum(nb * n - 1, 0), 0)),
                  step_blk(lambda b, n: (b * nstep + n, 0)),
                  one_blk(lambda b, n: (b * nblk + jnp.minimum(nb * n + nb, nblk - 1), 0)),
                  pl.BlockSpec((ctx, 512), lambda b, n: (ctx_blk0 + b, 0)),
                  lane_spec(2)],
        out_specs=step_blk(lambda b, n: (b * nstep + n, 0)),
        compiler_params=_params(nbytes, 2),
        name="gqa_window",
    )(sink, q, kv, kv, kv, kv, lane_masks)
    nbytes_c = 2 * 3 * ctx * 512 * 2 + 6 * (2 * ctx) * (2 * ctx) * 4
    o_ctx = pl.pallas_call(
        _gqa_ctx_kernel,
        out_shape=jax.ShapeDtypeStruct((rows - n_lat, 512), BF16),
        grid=(batch,),
        in_specs=[smem,
                  pl.BlockSpec((ctx, 512), lambda b: (ctx_blk0 + b, 0)),
                  pl.BlockSpec((ctx, 512), lambda b: (ctx_blk0 + b, 0)),
                  lane_spec(1)],
        out_specs=pl.BlockSpec((ctx, 512), lambda b: (b, 0)),
        compiler_params=_params(nbytes_c, 1),
        name="gqa_context",
    )(sink, q, kv, lane_masks)
    return o_lat, o_ctx


def _conv_kernel(yp_ref, yc_ref, yn_ref, w_ref, b_ref, g_ref, beta_ref, o_ref, ext_ref, acc_ref, *,
                 n_lat_tiles, lat_tiles_per_seq, ctx_tiles_per_seq):
    i = pl.program_id(0)
    tm, ch = yc_ref.shape
    is_lat = i < n_lat_tiles
    pos = jnp.where(is_lat, i % lat_tiles_per_seq, (i - n_lat_tiles) % ctx_tiles_per_seq)
    last_pos = jnp.where(is_lat, lat_tiles_per_seq - 1, ctx_tiles_per_seq - 1)
    keep_prev = (pos > 0).astype(F32)
    keep_next = (pos < last_pos).astype(F32)
    n_ext = tm + 2 * CONV_HALO
    pitch = n_ext // V7X_SUBLANES
    half = B_KERNEL // 2
    pad = CONV_HALO
    zero = jnp.zeros((pad, V7X_LANES), F32)
    for cb in range(ch // V7X_LANES):
        cs = slice(cb * V7X_LANES, (cb + 1) * V7X_LANES)
        ext_ref[cb, 0:pad, :] = zero
        ext_ref[cb, pad:pad + CONV_HALO, :] = yp_ref[:, cs] * keep_prev
        ext_ref[cb, pad + CONV_HALO:pad + CONV_HALO + tm, :] = yc_ref[:, cs]
        ext_ref[cb, pad + CONV_HALO + tm:pad + n_ext, :] = yn_ref[:, cs] * keep_next
        ext_ref[cb, pad + n_ext:, :] = jnp.zeros((ext_ref.shape[1] - pad - n_ext, V7X_LANES), F32)
    for cb in range(ch // V7X_LANES):
        cs = slice(cb * V7X_LANES, (cb + 1) * V7X_LANES)
        for k in range(pitch):
            acc = None
            for t in range(B_KERNEL):
                term = w_ref[t:t + 1, cs] * ext_ref[cb, pl.ds(pad + k + t - half, V7X_SUBLANES, stride=pitch), :]
                acc = term if acc is None else acc + term
            acc_ref[cb, pl.ds(k, V7X_SUBLANES, stride=pitch), :] = acc + b_ref[:, cs]
    y = jnp.concatenate([acc_ref[cb, CONV_HALO:CONV_HALO + tm, :] for cb in range(ch // V7X_LANES)], axis=1)
    mu = jnp.mean(y, axis=-1, keepdims=True)
    yc = y - mu
    var = jnp.mean(yc * yc, axis=-1, keepdims=True)
    z = yc * lax.rsqrt(var + LN_EPS) * g_ref[...] + beta_ref[...]
    o_ref[...] = (z * jax.nn.sigmoid(z)).astype(BF16)


def _conformer_conv(y, w_dw, b_dw, ln_g, ln_b, *, tm, seq, ctx, n_lat_rows):
    rows, ch = y.shape
    hb = tm // CONV_HALO
    n_halo_blocks = rows // CONV_HALO
    n_slab = ch // V7X_LANES
    n_ext = tm + 2 * CONV_HALO
    pitch = n_ext // V7X_SUBLANES
    assert n_ext % V7X_SUBLANES == 0 and pitch % 8 != 0 and B_KERNEL // 2 < CONV_HALO
    ext_rows = -(-(CONV_HALO + pitch + B_KERNEL // 2 + (V7X_SUBLANES - 1) * pitch) // V7X_SUBLANES) * V7X_SUBLANES
    kern = functools.partial(_conv_kernel, n_lat_tiles=n_lat_rows // tm,
                             lat_tiles_per_seq=seq // tm, ctx_tiles_per_seq=ctx // tm)
    nbytes = 2 * (tm + 2 * CONV_HALO) * ch * 4 + 2 * tm * ch * 2 + (2 * tm + 2 * CONV_HALO) * ch * 4 + 6 * tm * ch * 4
    vec = lambda: pl.BlockSpec((1, ch), lambda i: (0, 0))
    return pl.pallas_call(
        kern,
        out_shape=jax.ShapeDtypeStruct((rows, ch), BF16),
        grid=(rows // tm,),
        in_specs=[pl.BlockSpec((CONV_HALO, ch), lambda i: (jnp.maximum(i * hb - 1, 0), 0)),
                  pl.BlockSpec((tm, ch), lambda i: (i, 0)),
                  pl.BlockSpec((CONV_HALO, ch), lambda i: (jnp.minimum((i + 1) * hb, n_halo_blocks - 1), 0)),
                  pl.BlockSpec((B_KERNEL, ch), lambda i: (0, 0)),
                  vec(), vec(), vec()],
        out_specs=pl.BlockSpec((tm, ch), lambda i: (i, 0)),
        scratch_shapes=[pltpu.VMEM((n_slab, ext_rows, V7X_LANES), F32), pltpu.VMEM((n_slab, n_ext, V7X_LANES), F32)],
        compiler_params=_params(nbytes),
        name="conformer_conv",
    )(y, y, y, w_dw, b_dw.reshape(1, ch), ln_g.reshape(1, ch), ln_b.reshape(1, ch))


def _proj1_kernel(x_ref, mod_ref, g_ref, w1_ref, gq_ref, gkv_ref, wq_ref, wk_ref, wvt_ref, rope_ref,
                  q_ref, k_ref, vt_ref, *, q_scale):
    mod = mod_ref[0]
    xm = (_rms(x_ref[...], g_ref[1:2]) * (1.0 + mod[4:5]) + mod[3:4]).astype(BF16)
    c1 = jnp.dot(xm, w1_ref[...], preferred_element_type=F32)
    cos, s_next, s_prev = rope_ref[:, 0:128], rope_ref[:, 128:256], rope_ref[:, 256:384]
    cq = _rms(c1[:, :C_Q_LORA], gq_ref[...] * q_scale).astype(BF16)
    ckv = _rms(c1[:, C_Q_LORA:C_Q_LORA + C_KV_LORA], gkv_ref[...]).astype(BF16)
    kr = _rope128(c1[:, C_Q_LORA + C_KV_LORA:], cos, s_next, s_prev).astype(BF16)
    q = jnp.dot(cq, wq_ref[...], preferred_element_type=F32)
    kn = jnp.dot(ckv, wk_ref[...], preferred_element_type=F32)
    hp = C_HEAD_PAD
    for h in range(C_HEADS):
        q_ref[:, h * hp:h * hp + C_NOPE] = q[:, h * hp:h * hp + C_NOPE].astype(BF16)
        q_ref[:, h * hp + C_NOPE:(h + 1) * hp] = _rope128(
            q[:, h * hp + C_NOPE:(h + 1) * hp], cos, s_next, s_prev).astype(BF16)
        k_ref[:, h * hp:h * hp + C_NOPE] = kn[:, h * C_NOPE:(h + 1) * C_NOPE].astype(BF16)
        k_ref[:, h * hp + C_NOPE:(h + 1) * hp] = kr
    vt_ref[...] = lax.dot_general(wvt_ref[...], ckv, (((1,), (1,)), ((), ())),
                                  preferred_element_type=F32).astype(BF16)


def _proj1(x, mod, g3, w1, gq, gkv, wq, wk, wvt, rope_tab, *, tm, seq, n_lat_rows):
    rows, d = x.shape
    nb = mod.shape[0] - 1
    n_lat_tiles = n_lat_rows // tm
    tiles_per_seq = seq // tm
    rope_idx = lambda i: (jnp.where(i < n_lat_tiles, i % tiles_per_seq, tiles_per_seq), 0)
    qw = C_HEADS * C_HEAD_PAD
    vw = C_HEADS * C_V
    nbytes = (2 * (w1.size + wq.size + wk.size + wvt.size) + 2 * tm * (d * 4 + 384 * 4 + (2 * qw + vw) * 2)
              + tm * (w1.shape[1] + wq.shape[1] + wk.shape[1] + vw) * 4 + tm * d * 6)
    row = lambda w: pl.BlockSpec((tm, w), lambda i: (i, 0))
    return pl.pallas_call(
        functools.partial(_proj1_kernel, q_scale=(C_NOPE + C_ROPE) ** -0.5 * LOG2E),
        out_shape=(jax.ShapeDtypeStruct((rows, qw), BF16),
                   jax.ShapeDtypeStruct((rows, qw), BF16),
                   jax.ShapeDtypeStruct((C_HEADS * C_V, rows), BF16)),
        grid=(rows // tm,),
        in_specs=[row(d),
                  pl.BlockSpec((1, N_MOD, d), lambda i: (jnp.minimum(i * tm // seq, nb), 0, 0)),
                  _resident(g3.shape), _resident(w1.shape), _resident(gq.shape), _resident(gkv.shape),
                  _resident(wq.shape), _resident(wk.shape), _resident(wvt.shape),
                  pl.BlockSpec((tm, 384), rope_idx)],
        out_specs=(row(qw), row(qw), pl.BlockSpec((C_HEADS * C_V, tm), lambda i: (0, i))),
        compiler_params=_params(nbytes),
        name="mla_proj",
    )(x, mod, g3, w1, gq, gkv, wq, wk, wvt, rope_tab)


def _col_reduce(s, op, chains=8):
    r, n = s.shape
    s3 = s.reshape(r // V7X_SUBLANES, V7X_SUBLANES, n)
    chains = min(chains, s3.shape[0])
    step = s3.shape[0] // chains
    parts = []
    for g in range(chains):
        acc = s3[g * step]
        for i in range(g * step + 1, (g + 1) * step if g < chains - 1 else s3.shape[0]):
            acc = op(acc, s3[i])
        parts.append(acc)
    while len(parts) > 1:
        parts = [op(parts[i], parts[i + 1]) for i in range(0, len(parts) - 1, 2)] + (
            [parts[-1]] if len(parts) % 2 else [])
    return parts[0]


def _mla_kernel(q_ref, kl_ref, kc_ref, vtl_ref, vtc_ref, o_ref):
    hp = C_HEAD_PAD
    dn = (((1,), (1,)), ((), ()))

    def scores(h):
        qh = q_ref[:, h * hp:(h + 1) * hp]
        return (lax.dot_general(kl_ref[:, h * hp:(h + 1) * hp], qh, dn, preferred_element_type=F32),
                lax.dot_general(kc_ref[:, h * hp:(h + 1) * hp], qh, dn, preferred_element_type=F32))

    nxt = scores(0)
    for h in range(C_HEADS):
        s1, s2 = nxt
        if h + 1 < C_HEADS:
            nxt = scores(h + 1)
        m = jnp.max(jnp.maximum(_col_reduce(s1, jnp.maximum), _col_reduce(s2, jnp.maximum)),
                    axis=0, keepdims=True)
        e1 = jnp.exp2(s1 - m)
        e2 = jnp.exp2(s2 - m)
        l = jnp.sum(_col_reduce(e1, jnp.add) + _col_reduce(e2, jnp.add), axis=0, keepdims=True)
        ot = (jnp.dot(vtl_ref[h * C_V:(h + 1) * C_V, :], e1.astype(BF16), preferred_element_type=F32)
              + jnp.dot(vtc_ref[h * C_V:(h + 1) * C_V, :], e2.astype(BF16), preferred_element_type=F32))
        o_ref[:, h * C_V:(h + 1) * C_V] = (ot * (1.0 / l)).T.astype(BF16)


def _mla_attention(q, k, vt, *, batch, seq, ctx, tq):
    n_lat = batch * seq
    qw, vw, vpw = k.shape[1], C_HEADS * C_V, vt.shape[0]
    nq = seq // tq
    ctx_blk0 = n_lat // ctx
    nbytes = (2 * (seq + ctx) * (qw + vw) * 2 + 2 * tq * (qw + vw) * 2 + 4 * tq * (seq + ctx) * 4)
    return pl.pallas_call(
        _mla_kernel,
        out_shape=jax.ShapeDtypeStruct((n_lat, vw), BF16),
        grid=(batch, nq),
        in_specs=[pl.BlockSpec((tq, qw), lambda b, i: (b * nq + i, 0)),
                  pl.BlockSpec((seq, qw), lambda b, i: (b, 0)),
                  pl.BlockSpec((ctx, qw), lambda b, i: (ctx_blk0 + b, 0)),
                  pl.BlockSpec((vpw, seq), lambda b, i: (0, b)),
                  pl.BlockSpec((vpw, ctx), lambda b, i: (0, ctx_blk0 + b))],
        out_specs=pl.BlockSpec((tq, vw), lambda b, i: (b * nq + i, 0)),
        compiler_params=_params(nbytes, 2),
        name="mla_attention",
    )(q, k, k, vt, vt)


def _rope_table(seq, pad_rows):
    t = jnp.arange(seq)
    row = (t // GRID_W).astype(F32)
    col = (t % GRID_W).astype(F32)
    d_axis = A_HEAD_DIM // 2
    inv_freq = ROPE_BASE ** (-jnp.arange(0, d_axis, 2, dtype=F32) / d_axis)
    ang_r = row[:, None] * inv_freq
    ang_c = col[:, None] * inv_freq
    ang = jnp.concatenate([ang_r, ang_r, ang_c, ang_c], axis=-1)
    cos, sin = jnp.cos(ang), jnp.sin(ang)
    first_half = (jnp.arange(A_HEAD_DIM) % 32) < 16
    s_next = jnp.where(first_half, -sin, 0.0)
    s_prev = jnp.where(first_half, 0.0, sin)
    tab = jnp.concatenate([jnp.tile(cos, (1, 2)), jnp.tile(s_next, (1, 2)), jnp.tile(s_prev, (1, 2))], axis=-1)
    ident = jnp.concatenate([jnp.ones((pad_rows, 128), F32), jnp.zeros((pad_rows, 256), F32)], axis=-1)
    return jnp.concatenate([tab, ident], axis=0)


def kernel(x, c, ctx, c_ctx, w_mod, b_mod, g_norm, ffn_w_in, ffn_w_out, ab_w_in, a_sink, b_w_dw, b_b_dw,
           b_ln_g, b_ln_b, ab_w_out, c_w_dq, c_g_q, c_w_uq, c_w_dkv, c_g_kv, c_w_uk, c_w_uv, c_w_o, g_final):
    batch, seq, d = x.shape
    ctx_len = ctx.shape[1]
    depth = w_mod.shape[0]
    d_ff = ffn_w_out.shape[2]
    n_lat = batch * seq
    rows = n_lat + batch * ctx_len
    tm = min(512, seq)
    tm_conv = min(256, ctx_len)
    tq = min(2 * V7X_MXU_DIM, seq)
    ck = V7X_MXU_DIM
    assert seq % tm == 0 and (batch * ctx_len) % tm == 0 and seq % A_BLOCK == 0 and seq % GRID_W == 0
    assert ctx_len % tm_conv == 0 and seq % tm_conv == 0 and n_lat % ctx_len == 0 and d_ff % ck == 0
    assert depth % 2 == 0 and depth <= 2

    r_mod = -(-(batch + 1) // 16) * 16
    cc = jnp.concatenate([c, c_ctx[None, :], jnp.zeros((r_mod - batch - 1, d), F32)], axis=0)
    mod_all = _modulation(cc, w_mod, b_mod)[:, :batch + 1].reshape(depth, batch + 1, N_MOD, d)

    rope_tab = _rope_table(seq, tm)
    lane_masks = (jnp.arange(128)[None, :] // 64 == jnp.arange(2)[:, None]).astype(BF16)
    t = (x.reshape(n_lat, d), ctx.reshape(batch * ctx_len, d))
    ffn_win, ffn_wout = ffn_w_in.astype(BF16), ffn_w_out.astype(BF16)

    for i in range(depth):
        last = i == depth - 1
        j = i // 2
        mod = mod_all[i]
        g3 = g_norm[i]
        t = _ffn(t, mod, g3, ffn_win, ffn_wout, (i, 0), rows=rows, tm=tm, seq=seq, n_lat_rows=n_lat,
                 sub=0, ck=ck)
        if i % 2 == 0:
            w = ab_w_in[j]
            iq, ik = A_HEADS * A_HEAD_DIM, A_KV_HEADS * A_HEAD_DIM
            wk, wv = w[:, iq:iq + ik], w[:, iq + ik:iq + 2 * ik]
            swap = lambda m: jnp.concatenate([m[:, A_HEAD_DIM:], m[:, :A_HEAD_DIM]], axis=1)
            w_all = jnp.concatenate([w[:, :iq], wk, swap(wk), wv, swap(wv), w[:, iq + 2 * ik:]], axis=1).astype(BF16)
            q, kv, y = _proj0(t, mod, g3, w_all, rope_tab, tm=tm, seq=seq, n_lat_rows=n_lat)
            a = _gqa(q, kv, a_sink[j].astype(F32), lane_masks, batch=batch, seq=seq, ctx=ctx_len)
            bx = _conformer_conv(y, b_w_dw[j], b_b_dw[j], b_ln_g[j], b_ln_b[j],
                                 tm=tm_conv, seq=seq, ctx=ctx_len, n_lat_rows=n_lat)
            wo = ab_w_out[j].astype(BF16)
            attn, wos = (a, bx), (wo[:iq], wo[iq:])
        else:
            hp = C_HEAD_PAD
            wdkv = c_w_dkv[j]
            w1 = jnp.concatenate([c_w_dq[j], wdkv, jnp.zeros((d, 128 - C_ROPE), F32)], axis=1).astype(BF16)
            wuq = c_w_uq[j].reshape(C_Q_LORA, C_HEADS, C_NOPE + C_ROPE)
            wuq = jnp.pad(wuq, ((0, 0), (0, 0), (0, hp - C_NOPE - C_ROPE))).reshape(C_Q_LORA, C_HEADS * hp)
            q, k, vt = _proj1(t, mod, g3, w1, c_g_q[j].reshape(1, -1), c_g_kv[j].reshape(1, -1),
                              wuq.astype(BF16), c_w_uk[j].astype(BF16), c_w_uv[j].T.astype(BF16), rope_tab,
                              tm=tm, seq=seq, n_lat_rows=n_lat)
            a = _mla_attention(q, k, vt, batch=batch, seq=seq, ctx=ctx_len, tq=tq)
            attn, wos = (a,), (c_w_o[j].astype(BF16),)
        t = _ffn(t, mod, g3, ffn_win, ffn_wout, (i, 1), rows=n_lat if last else rows, tm=tm, seq=seq,
                 n_lat_rows=n_lat, sub=2, ck=ck, attn=attn, wo=wos, g_final=g_final if last else None)
    return t.reshape(batch, seq, d)
```

```python
import functools
import math

import jax
import jax.numpy as jnp
from jax import lax
from jax.experimental import pallas as pl
from jax.experimental.pallas import tpu as pltpu

F32 = jnp.float32
BF16 = jnp.bfloat16

GRID_W = 64
ROPE_BASE = 10000.0
NORM_EPS = 1e-6
LN_EPS = 1e-5
NEG_INF = -1e30
LOG2E = math.log2(math.e)
N_MOD = 9
FFN_RES = 0.5
A_HEADS = 8
A_KV_HEADS = 2
A_HEAD_DIM = 64
A_WINDOW = 128
A_BLOCK = 128
B_CH = 512
B_KERNEL = 31
C_HEADS = 8
C_Q_LORA = 256
C_KV_LORA = 256
C_NOPE = 128
C_ROPE = 64
C_V = 128

V7X_LANES = 128
V7X_SUBLANES = 8
V7X_MXU_DIM = 256
V7X_VMEM_BYTES = 64 * 1024 * 1024
V7X_VMEM_USABLE = 56 * 1024 * 1024

C_HEAD_PAD = V7X_MXU_DIM
CONV_HALO = 16
GQA_BLOCKS = 8


def _vmem_limit(nbytes):
    return int(min(V7X_VMEM_USABLE, max(16 * 1024 * 1024, nbytes * 3 // 2)))


def _params(nbytes, ngrid=1):
    return pltpu.CompilerParams(dimension_semantics=("arbitrary",) * ngrid,
                                vmem_limit_bytes=_vmem_limit(nbytes))


def _rms(x, g):
    return x * lax.rsqrt(jnp.mean(x * x, axis=-1, keepdims=True) + NORM_EPS) * g


def _resident(shape, lead=()):
    nd = len(shape) - len(lead)
    return pl.BlockSpec((None,) * len(lead) + tuple(shape[len(lead):]),
                        lambda *_: tuple(lead) + (0,) * nd, pipeline_mode=pl.Buffered(1))


def _mod_kernel(c_ref, w_ref, b_ref, o_ref):
    c = c_ref[...]
    a = (c * jax.nn.sigmoid(c)).astype(BF16)
    o_ref[0] = jnp.dot(a, w_ref[0].astype(BF16), preferred_element_type=F32) + b_ref[0]


def _modulation(cc, w_mod, b_mod):
    depth, d, n = w_mod.shape
    r = cc.shape[0]
    tn = n // N_MOD
    nbytes = 2 * (d * tn * 4) + d * tn * 2 + 4 * r * (d + 2 * tn) * 4
    return pl.pallas_call(
        _mod_kernel,
        out_shape=jax.ShapeDtypeStruct((depth, r, n), F32),
        grid=(depth, n // tn),
        in_specs=[pl.BlockSpec((r, d), lambda l, j: (0, 0)),
                  pl.BlockSpec((1, d, tn), lambda l, j: (l, 0, j)),
                  pl.BlockSpec((1, 1, tn), lambda l, j: (l, 0, j))],
        out_specs=pl.BlockSpec((1, r, tn), lambda l, j: (l, 0, j)),
        compiler_params=_params(nbytes, 2),
        name="adaln_modulation",
    )(cc, w_mod, b_mod.reshape(depth, 1, n))


def _row_specs(op, tm, n_lat_tiles):
    if not isinstance(op, tuple):
        return [pl.BlockSpec((tm, op.shape[1]), lambda i: (i, 0))]
    lat, cx = op
    return [pl.BlockSpec((tm, lat.shape[1]), lambda i: (jnp.minimum(i, n_lat_tiles - 1), 0)),
            pl.BlockSpec((tm, cx.shape[1]), lambda i: (jnp.maximum(i - n_lat_tiles, 0), 0))]


def _row_load(refs, is_lat):
    if len(refs) == 1:
        return refs[0][...]
    return jnp.where(is_lat, refs[0][...], refs[1][...])


def _ffn_kernel(*refs, arity, sub, final_norm, ck, n_lat_tiles):
    refs = list(refs)
    o_ref = refs.pop()
    take = lambda n: [refs.pop(0) for _ in range(n)]
    x_refs = take(arity[0])
    mod_ref, g_ref = take(2)
    a_refs = [take(n) for n in arity[1:]]
    wo_refs = take(len(arity) - 1)
    win_ref, wout_ref = take(2)
    gf_ref = refs.pop(0) if final_norm else None
    is_lat = pl.program_id(0) < n_lat_tiles

    x = _row_load(x_refs, is_lat)
    mod = mod_ref[0]
    if a_refs:
        ox = None
        for ar, wo_ref in zip(a_refs, wo_refs):
            part = jnp.dot(_row_load(ar, is_lat), wo_ref[...], preferred_element_type=F32)
            ox = part if ox is None else ox + part
        x = x + mod[5:6] * ox
    shift, scale, gate = mod[3 * sub:3 * sub + 1], mod[3 * sub + 1:3 * sub + 2], mod[3 * sub + 2:3 * sub + 3]
    h = (_rms(x, g_ref[sub:sub + 1]) * (1.0 + scale) + shift).astype(BF16)
    d_ff = wout_ref.shape[0]
    acc = None
    for c in range(d_ff // ck):
        g = jnp.dot(h, win_ref[:, c * ck:(c + 1) * ck].astype(BF16), preferred_element_type=F32)
        u = jnp.dot(h, win_ref[:, d_ff + c * ck:d_ff + (c + 1) * ck].astype(BF16), preferred_element_type=F32)
        act = (g * jax.nn.sigmoid(g) * u).astype(BF16)
        part = jnp.dot(act, wout_ref[c * ck:(c + 1) * ck, :].astype(BF16), preferred_element_type=F32)
        acc = part if acc is None else acc + part
    y = x + (FFN_RES * gate) * acc
    if final_norm:
        y = _rms(y, gf_ref[...])
    o_ref[...] = y


def _ffn(x, mod, g3, win, wout, widx, *, rows, tm, seq, n_lat_rows, sub, ck, attn=(), wo=(), g_final=None):
    d = win.shape[-2]
    nb = mod.shape[0] - 1
    n_lat_tiles = n_lat_rows // tm
    arity = tuple(2 if isinstance(o, tuple) else 1 for o in (x, *attn))
    flat = lambda o: list(o) if isinstance(o, tuple) else [o]
    in_specs = _row_specs(x, tm, n_lat_tiles)
    in_specs += [pl.BlockSpec((1, N_MOD, d), lambda i: (jnp.minimum(i * tm // seq, nb), 0, 0)),
                 _resident(g3.shape)]
    for a in attn:
        in_specs += _row_specs(a, tm, n_lat_tiles)
    in_specs += [_resident(w.shape) for w in wo]
    in_specs += [_resident(win.shape, widx), _resident(wout.shape, widx)]
    args = [*flat(x), mod, g3, *[m for a in attn for m in flat(a)], *wo, win, wout]
    if g_final is not None:
        in_specs.append(_resident((1, d)))
        args.append(g_final.reshape(1, d))
    weights = (win.dtype.itemsize * (math.prod(win.shape[-2:]) + math.prod(wout.shape[-2:]))
               + 2 * sum(w.size for w in wo))
    tiles = 2 * tm * (arity[0] * d * 4 + d * 4 + sum(n * flat(a)[0].shape[1] for n, a in zip(arity[1:], attn)) * 2)
    temps = tm * (d * 4 * 3 + d * 2 + 2 * ck * 4 * 2 + ck * 2)
    kern = functools.partial(_ffn_kernel, arity=arity, sub=sub, final_norm=g_final is not None,
                             ck=ck, n_lat_tiles=n_lat_tiles)
    return pl.pallas_call(
        kern,
        out_shape=jax.ShapeDtypeStruct((rows, d), F32),
        grid=(rows // tm,),
        in_specs=in_specs,
        out_specs=pl.BlockSpec((tm, d), lambda i: (i, 0)),
        compiler_params=_params(weights + tiles + temps),
        name=f"ffn_sub{sub}" + ("_mix" if attn else "") + ("_final" if g_final is not None else ""),
    )(*args)


def _rope128(v, cos, sin_next, sin_prev):
    return (v * cos + pltpu.roll(v, V7X_LANES - 16, 1) * sin_next + pltpu.roll(v, 16, 1) * sin_prev)


def _proj0_kernel(x_ref, mod_ref, g_ref, w_ref, rope_ref, q_ref, kv_ref, y_ref, *, q_scale):
    mod = mod_ref[0]
    xm = (_rms(x_ref[...], g_ref[1:2]) * (1.0 + mod[4:5]) + mod[3:4]).astype(BF16)
    nq = A_HEADS * A_HEAD_DIM // V7X_LANES
    base = nq * 128
    p_qk = jnp.dot(xm, w_ref[:, :base + 256], preferred_element_type=F32)
    p_ag = jnp.dot(xm, w_ref[:, base + 512:], preferred_element_type=F32)
    cos, s_next, s_prev = rope_ref[:, 0:128], rope_ref[:, 128:256], rope_ref[:, 256:384]
    for j in range(nq):
        blk = p_qk[:, j * 128:(j + 1) * 128]
        q_ref[:, j * 128:(j + 1) * 128] = (_rope128(blk, cos, s_next, s_prev) * q_scale).astype(BF16)
    for j in range(2):
        blk = p_qk[:, base + j * 128:base + (j + 1) * 128]
        kv_ref[:, j * 128:(j + 1) * 128] = _rope128(blk, cos, s_next, s_prev).astype(BF16)
    p_v = jnp.dot(xm, w_ref[:, base + 256:base + 512], preferred_element_type=F32)
    y_ref[...] = p_ag[:, :B_CH] * jax.nn.sigmoid(p_ag[:, B_CH:])
    kv_ref[:, 256:512] = p_v.astype(BF16)


def _proj0(x, mod, g3, w, rope_tab, *, tm, seq, n_lat_rows):
    rows, d = x.shape
    nb = mod.shape[0] - 1
    n_lat_tiles = n_lat_rows // tm
    tiles_per_seq = seq // tm
    rope_idx = lambda i: (jnp.where(i < n_lat_tiles, i % tiles_per_seq, tiles_per_seq), 0)
    nw = w.shape[1]
    nbytes = 2 * w.size + 2 * tm * (d * 4 + 384 * 4 + 1024 * 2 + 512 * 4) + tm * (nw * 4 + d * 6)
    return pl.pallas_call(
        functools.partial(_proj0_kernel, q_scale=A_HEAD_DIM ** -0.5 * LOG2E),
        out_shape=(jax.ShapeDtypeStruct((rows, 512), BF16),
                   jax.ShapeDtypeStruct((rows, 512), BF16),
                   jax.ShapeDtypeStruct((rows, B_CH), F32)),
        grid=(rows // tm,),
        in_specs=[pl.BlockSpec((tm, d), lambda i: (i, 0)),
                  pl.BlockSpec((1, N_MOD, d), lambda i: (jnp.minimum(i * tm // seq, nb), 0, 0)),
                  _resident(g3.shape), _resident(w.shape),
                  pl.BlockSpec((tm, 384), rope_idx)],
        out_specs=(pl.BlockSpec((tm, 512), lambda i: (i, 0)),
                   pl.BlockSpec((tm, 512), lambda i: (i, 0)),
                   pl.BlockSpec((tm, B_CH), lambda i: (i, 0))),
        compiler_params=_params(nbytes),
        name="ab_in_proj",
    )(x, mod, g3, w, rope_tab)


def _gqa_core(q_ref, kv_blocks, groups, lane_ref, sink_ref, o_ref):
    m_lo, m_hi = lane_ref[0:1, :], lane_ref[1:2, :]
    f_lo, f_hi = m_lo.astype(F32), m_hi.astype(F32)
    k_var, v_var = [], []
    for blk in kv_blocks:
        k, k_sw, v, v_sw = blk[:, 0:128], blk[:, 128:256], blk[:, 256:384], blk[:, 384:512]
        k_var.append(((k * m_lo, k_sw * m_hi), (k_sw * m_lo, k * m_hi)))
        v_var.append(((v * m_lo, v_sw * m_hi), (v_sw * m_lo, v * m_hi)))
    chains = [(g, hk) for g in groups for hk in range(A_KV_HEADS)]

    def scores(group, hk):
        row0, r, ids, _ = group
        qs = jnp.concatenate([q_ref[row0:row0 + r, (2 * hk) * 128:(2 * hk + 1) * 128],
                              q_ref[row0:row0 + r, (2 * hk + 1) * 128:(2 * hk + 2) * 128]], axis=0)
        kcat = jnp.concatenate([k_var[b][hk][0] for b in ids] + [k_var[b][hk][1] for b in ids], axis=0)
        return lax.dot_general(qs, kcat, (((1,), (1,)), ((), ())), preferred_element_type=F32)

    all_scores = [scores(g, hk) for g, hk in chains]
    for (group, hk), s in zip(chains, all_scores):
        row0, r, ids, mask = group
        nk = s.shape[1] // 2
        vcat = jnp.concatenate([v_var[b][hk][0] for b in ids] + [v_var[b][hk][1] for b in ids], axis=0)
        ps, rinv = [], []
        for j in range(2):
            pj, rj = [], []
            for par in range(2):
                sj = s[j * r:(j + 1) * r, par * nk:(par + 1) * nk]
                if mask is not None:
                    sj = jnp.where(mask, sj, NEG_INF)
                sink = sink_ref[4 * hk + 2 * j + par] * LOG2E
                m = jnp.maximum(jnp.max(sj, axis=-1, keepdims=True), sink)
                e = jnp.exp2(sj - m)
                l = jnp.sum(e, axis=-1, keepdims=True) + jnp.exp2(sink - m)
                pj.append(e.astype(BF16))
                rj.append(1.0 / l)
            ps.append(jnp.concatenate(pj, axis=1))
            rinv.append(rj[0] * f_lo + rj[1] * f_hi)
        p = jnp.concatenate(ps, axis=0)
        o = jnp.dot(p, vcat, preferred_element_type=F32)
        for j in range(2):
            blk = 2 * hk + j
            o_ref[row0:row0 + r, blk * 128:(blk + 1) * 128] = (o[j * r:(j + 1) * r] * rinv[j]).astype(BF16)


def _gqa_window_kernel(sink_ref, q_ref, kvp_ref, kvc_ref, kvn_ref, kvx_ref, lane_ref, o_ref, *, nstep):
    n = pl.program_id(1)
    nb = q_ref.shape[0] // A_BLOCK
    ctx = kvx_ref.shape[0]
    nk = 3 * A_BLOCK + ctx
    i = lax.broadcasted_iota(jnp.int32, (A_BLOCK, nk), 0)
    j = lax.broadcasted_iota(jnp.int32, (A_BLOCK, nk), 1)
    band = (j >= i) & (j <= i + 2 * A_WINDOW)
    is_ctx = j >= 3 * A_BLOCK
    lo = jnp.where(n > 0, 0, A_BLOCK)
    hi = jnp.where(n < nstep - 1, 3 * A_BLOCK, 2 * A_BLOCK)
    mask_mid = band | is_ctx
    cur = kvc_ref[...]
    blocks = [kvp_ref[...]] + [cur[g * A_BLOCK:(g + 1) * A_BLOCK] for g in range(nb)] + [kvn_ref[...], kvx_ref[...]]
    groups = []
    for g in range(nb):
        mask = mask_mid
        if g == 0:
            mask = (band & (j >= lo)) | is_ctx
        if g == nb - 1:
            mask = (band & (j < hi)) | is_ctx if g else (band & (j >= lo) & (j < hi)) | is_ctx
        groups.append((g * A_BLOCK, A_BLOCK, (g, g + 1, g + 2, nb + 2), mask))
    _gqa_core(q_ref, blocks, groups, lane_ref, sink_ref, o_ref)


def _gqa_ctx_kernel(sink_ref, q_ref, kvx_ref, lane_ref, o_ref):
    _gqa_core(q_ref, [kvx_ref[...]], [(0, q_ref.shape[0], (0,), None)], lane_ref, sink_ref, o_ref)


def _gqa(q, kv, sink, lane_masks, *, batch, seq, ctx):
    rows = q.shape[0]
    nblk = seq // A_BLOCK
    n_lat = batch * seq
    ctx_blk0 = n_lat // ctx
    smem = pl.BlockSpec(memory_space=pltpu.SMEM)
    lane_spec = lambda nd: pl.BlockSpec((2, 128), lambda *_: (0, 0))
    nb = GQA_BLOCKS if nblk % GQA_BLOCKS == 0 else 1
    nstep = nblk // nb
    one_blk = lambda f: pl.BlockSpec((A_BLOCK, 512), f)
    step_blk = lambda f: pl.BlockSpec((nb * A_BLOCK, 512), f)
    nbytes = 2 * ((2 * nb + 2) * A_BLOCK + ctx) * 512 * 2 + 5 * nb * (2 * A_BLOCK) * 2 * (3 * A_BLOCK + ctx) * 4
    o_lat = pl.pallas_call(
        functools.partial(_gqa_window_kernel, nstep=nstep),
        out_shape=jax.ShapeDtypeStruct((n_lat, 512), BF16),
        grid=(batch, nstep),
        in_specs=[smem,
                  step_blk(lambda b, n: (b * nstep + n, 0)),
                  one_blk(lambda b, n: (b * nblk + jnp.maximum(nb * n - 1, 0), 0)),
                  step_blk(lambda b, n: (b * nstep + n, 0)),
                  one_blk(lambda b, n: (b * nblk + jnp.minimum(nb * n + nb, nblk - 1), 0)),
                  pl.BlockSpec((ctx, 512), lambda b, n: (ctx_blk0 + b, 0)),
                  lane_spec(2)],
        out_specs=step_blk(lambda b, n: (b * nstep + n, 0)),
        compiler_params=_params(nbytes, 2),
        name="gqa_window",
    )(sink, q, kv, kv, kv, kv, lane_masks)
    nbytes_c = 2 * 3 * ctx * 512 * 2 + 6 * (2 * ctx) * (2 * ctx) * 4
    o_ctx = pl.pallas_call(
        _gqa_ctx_kernel,
        out_shape=jax.ShapeDtypeStruct((rows - n_lat, 512), BF16),
        grid=(batch,),
        in_specs=[smem,
                  pl.BlockSpec((ctx, 512), lambda b: (ctx_blk0 + b, 0)),
                  pl.BlockSpec((ctx, 512), lambda b: (ctx_blk0 + b, 0)),
                  lane_spec(1)],
        out_specs=pl.BlockSpec((ctx, 512), lambda b: (b, 0)),
        compiler_params=_params(nbytes_c, 1),
        name="gqa_context",
    )(sink, q, kv, lane_masks)
    return o_lat, o_ctx


def _conv_kernel(yp_ref, yc_ref, yn_ref, w_ref, b_ref, g_ref, beta_ref, o_ref, ext_ref, acc_ref, *,
                 n_lat_tiles, lat_tiles_per_seq, ctx_tiles_per_seq):
    i = pl.program_id(0)
    tm, ch = yc_ref.shape
    is_lat = i < n_lat_tiles
    pos = jnp.where(is_lat, i % lat_tiles_per_seq, (i - n_lat_tiles) % ctx_tiles_per_seq)
    last_pos = jnp.where(is_lat, lat_tiles_per_seq - 1, ctx_tiles_per_seq - 1)
    keep_prev = (pos > 0).astype(F32)
    keep_next = (pos < last_pos).astype(F32)
    n_ext = tm + 2 * CONV_HALO
    pitch = n_ext // V7X_SUBLANES
    half = B_KERNEL // 2
    pad = CONV_HALO
    zero = jnp.zeros((pad, V7X_LANES), F32)
    for cb in range(ch // V7X_LANES):
        cs = slice(cb * V7X_LANES, (cb + 1) * V7X_LANES)
        ext_ref[cb, 0:pad, :] = zero
        ext_ref[cb, pad:pad + CONV_HALO, :] = yp_ref[:, cs] * keep_prev
        ext_ref[cb, pad + CONV_HALO:pad + CONV_HALO + tm, :] = yc_ref[:, cs]
        ext_ref[cb, pad + CONV_HALO + tm:pad + n_ext, :] = yn_ref[:, cs] * keep_next
        ext_ref[cb, pad + n_ext:, :] = jnp.zeros((ext_ref.shape[1] - pad - n_ext, V7X_LANES), F32)
    for cb in range(ch // V7X_LANES):
        cs = slice(cb * V7X_LANES, (cb + 1) * V7X_LANES)
        for k in range(pitch):
            acc = None
            for t in range(B_KERNEL):
                term = w_ref[t:t + 1, cs] * ext_ref[cb, pl.ds(pad + k + t - half, V7X_SUBLANES, stride=pitch), :]
                acc = term if acc is None else acc + term
            acc_ref[cb, pl.ds(k, V7X_SUBLANES, stride=pitch), :] = acc + b_ref[:, cs]
    y = jnp.concatenate([acc_ref[cb, CONV_HALO:CONV_HALO + tm, :] for cb in range(ch // V7X_LANES)], axis=1)
    mu = jnp.mean(y, axis=-1, keepdims=True)
    yc = y - mu
    var = jnp.mean(yc * yc, axis=-1, keepdims=True)
    z = yc * lax.rsqrt(var + LN_EPS) * g_ref[...] + beta_ref[...]
    o_ref[...] = (z * jax.nn.sigmoid(z)).astype(BF16)


def _conformer_conv(y, w_dw, b_dw, ln_g, ln_b, *, tm, seq, ctx, n_lat_rows):
    rows, ch = y.shape
    hb = tm // CONV_HALO
    n_halo_blocks = rows // CONV_HALO
    n_slab = ch // V7X_LANES
    n_ext = tm + 2 * CONV_HALO
    pitch = n_ext // V7X_SUBLANES
    assert n_ext % V7X_SUBLANES == 0 and pitch % 8 != 0 and B_KERNEL // 2 < CONV_HALO
    ext_rows = -(-(CONV_HALO + pitch + B_KERNEL // 2 + (V7X_SUBLANES - 1) * pitch) // V7X_SUBLANES) * V7X_SUBLANES
    kern = functools.partial(_conv_kernel, n_lat_tiles=n_lat_rows // tm,
                             lat_tiles_per_seq=seq // tm, ctx_tiles_per_seq=ctx // tm)
    nbytes = 2 * (tm + 2 * CONV_HALO) * ch * 4 + 2 * tm * ch * 2 + (2 * tm + 2 * CONV_HALO) * ch * 4 + 6 * tm * ch * 4
    vec = lambda: pl.BlockSpec((1, ch), lambda i: (0, 0))
    return pl.pallas_call(
        kern,
        out_shape=jax.ShapeDtypeStruct((rows, ch), BF16),
        grid=(rows // tm,),
        in_specs=[pl.BlockSpec((CONV_HALO, ch), lambda i: (jnp.maximum(i * hb - 1, 0), 0)),
                  pl.BlockSpec((tm, ch), lambda i: (i, 0)),
                  pl.BlockSpec((CONV_HALO, ch), lambda i: (jnp.minimum((i + 1) * hb, n_halo_blocks - 1), 0)),
                  pl.BlockSpec((B_KERNEL, ch), lambda i: (0, 0)),
                  vec(), vec(), vec()],
        out_specs=pl.BlockSpec((tm, ch), lambda i: (i, 0)),
        scratch_shapes=[pltpu.VMEM((n_slab, ext_rows, V7X_LANES), F32), pltpu.VMEM((n_slab, n_ext, V7X_LANES), F32)],
        compiler_params=_params(nbytes),
        name="conformer_conv",
    )(y, y, y, w_dw, b_dw.reshape(1, ch), ln_g.reshape(1, ch), ln_b.reshape(1, ch))


def _proj1_kernel(x_ref, mod_ref, g_ref, w1_ref, gq_ref, gkv_ref, wq_ref, wk_ref, wvt_ref, rope_ref,
                  q_ref, k_ref, vt_ref, *, q_scale):
    mod = mod_ref[0]
    xm = (_rms(x_ref[...], g_ref[1:2]) * (1.0 + mod[4:5]) + mod[3:4]).astype(BF16)
    c1 = jnp.dot(xm, w1_ref[...], preferred_element_type=F32)
    cos, s_next, s_prev = rope_ref[:, 0:128], rope_ref[:, 128:256], rope_ref[:, 256:384]
    cq = _rms(c1[:, :C_Q_LORA], gq_ref[...] * q_scale).astype(BF16)
    ckv = _rms(c1[:, C_Q_LORA:C_Q_LORA + C_KV_LORA], gkv_ref[...]).astype(BF16)
    kr = _rope128(c1[:, C_Q_LORA + C_KV_LORA:], cos, s_next, s_prev).astype(BF16)
    q = jnp.dot(cq, wq_ref[...], preferred_element_type=F32)
    kn = jnp.dot(ckv, wk_ref[...], preferred_element_type=F32)
    hp = C_HEAD_PAD
    for h in range(C_HEADS):
        q_ref[:, h * hp:h * hp + C_NOPE] = q[:, h * hp:h * hp + C_NOPE].astype(BF16)
        q_ref[:, h * hp + C_NOPE:(h + 1) * hp] = _rope128(
            q[:, h * hp + C_NOPE:(h + 1) * hp], cos, s_next, s_prev).astype(BF16)
        k_ref[:, h * hp:h * hp + C_NOPE] = kn[:, h * C_NOPE:(h + 1) * C_NOPE].astype(BF16)
        k_ref[:, h * hp + C_NOPE:(h + 1) * hp] = kr
    vt_ref[...] = lax.dot_general(wvt_ref[...], ckv, (((1,), (1,)), ((), ())),
                                  preferred_element_type=F32).astype(BF16)


def _proj1(x, mod, g3, w1, gq, gkv, wq, wk, wvt, rope_tab, *, tm, seq, n_lat_rows):
    rows, d = x.shape
    nb = mod.shape[0] - 1
    n_lat_tiles = n_lat_rows // tm
    tiles_per_seq = seq // tm
    rope_idx = lambda i: (jnp.where(i < n_lat_tiles, i % tiles_per_seq, tiles_per_seq), 0)
    qw = C_HEADS * C_HEAD_PAD
    vw = C_HEADS * C_V
    nbytes = (2 * (w1.size + wq.size + wk.size + wvt.size) + 2 * tm * (d * 4 + 384 * 4 + (2 * qw + vw) * 2)
              + tm * (w1.shape[1] + wq.shape[1] + wk.shape[1] + vw) * 4 + tm * d * 6)
    row = lambda w: pl.BlockSpec((tm, w), lambda i: (i, 0))
    return pl.pallas_call(
        functools.partial(_proj1_kernel, q_scale=(C_NOPE + C_ROPE) ** -0.5 * LOG2E),
        out_shape=(jax.ShapeDtypeStruct((rows, qw), BF16),
                   jax.ShapeDtypeStruct((rows, qw), BF16),
                   jax.ShapeDtypeStruct((C_HEADS * C_V, rows), BF16)),
        grid=(rows // tm,),
        in_specs=[row(d),
                  pl.BlockSpec((1, N_MOD, d), lambda i: (jnp.minimum(i * tm // seq, nb), 0, 0)),
                  _resident(g3.shape), _resident(w1.shape), _resident(gq.shape), _resident(gkv.shape),
                  _resident(wq.shape), _resident(wk.shape), _resident(wvt.shape),
                  pl.BlockSpec((tm, 384), rope_idx)],
        out_specs=(row(qw), row(qw), pl.BlockSpec((C_HEADS * C_V, tm), lambda i: (0, i))),
        compiler_params=_params(nbytes),
        name="mla_proj",
    )(x, mod, g3, w1, gq, gkv, wq, wk, wvt, rope_tab)


def _col_reduce(s, op, chains=8):
    r, n = s.shape
    s3 = s.reshape(r // V7X_SUBLANES, V7X_SUBLANES, n)
    chains = min(chains, s3.shape[0])
    step = s3.shape[0] // chains
    parts = []
    for g in range(chains):
        acc = s3[g * step]
        for i in range(g * step + 1, (g + 1) * step if g < chains - 1 else s3.shape[0]):
            acc = op(acc, s3[i])
        parts.append(acc)
    while len(parts) > 1:
        parts = [op(parts[i], parts[i + 1]) for i in range(0, len(parts) - 1, 2)] + (
            [parts[-1]] if len(parts) % 2 else [])
    return parts[0]


def _mla_kernel(q_ref, kl_ref, kc_ref, vtl_ref, vtc_ref, o_ref):
    hp = C_HEAD_PAD
    dn = (((1,), (1,)), ((), ()))

    def scores(h):
        qh = q_ref[:, h * hp:(h + 1) * hp]
        return (lax.dot_general(kl_ref[:, h * hp:(h + 1) * hp], qh, dn, preferred_element_type=F32),
                lax.dot_general(kc_ref[:, h * hp:(h + 1) * hp], qh, dn, preferred_element_type=F32))

    nxt = scores(0)
    for h in range(C_HEADS):
        s1, s2 = nxt
        if h + 1 < C_HEADS:
            nxt = scores(h + 1)
        m = jnp.max(jnp.maximum(_col_reduce(s1, jnp.maximum), _col_reduce(s2, jnp.maximum)),
                    axis=0, keepdims=True)
        e1 = jnp.exp2(s1 - m)
        e2 = jnp.exp2(s2 - m)
        l = jnp.sum(_col_reduce(e1, jnp.add) + _col_reduce(e2, jnp.add), axis=0, keepdims=True)
        ot = (jnp.dot(vtl_ref[h * C_V:(h + 1) * C_V, :], e1.astype(BF16), preferred_element_type=F32)
              + jnp.dot(vtc_ref[h * C_V:(h + 1) * C_V, :], e2.astype(BF16), preferred_element_type=F32))
        o_ref[:, h * C_V:(h + 1) * C_V] = (ot * (1.0 / l)).T.astype(BF16)


def _mla_attention(q, k, vt, *, batch, seq, ctx, tq):
    n_lat = batch * seq
    qw, vw, vpw = k.shape[1], C_HEADS * C_V, vt.shape[0]
    nq = seq // tq
    ctx_blk0 = n_lat // ctx
    nbytes = (2 * (seq + ctx) * (qw + vw) * 2 + 2 * tq * (qw + vw) * 2 + 4 * tq * (seq + ctx) * 4)
    return pl.pallas_call(
        _mla_kernel,
        out_shape=jax.ShapeDtypeStruct((n_lat, vw), BF16),
        grid=(batch, nq),
        in_specs=[pl.BlockSpec((tq, qw), lambda b, i: (b * nq + i, 0)),
                  pl.BlockSpec((seq, qw), lambda b, i: (b, 0)),
                  pl.BlockSpec((ctx, qw), lambda b, i: (ctx_blk0 + b, 0)),
                  pl.BlockSpec((vpw, seq), lambda b, i: (0, b)),
                  pl.BlockSpec((vpw, ctx), lambda b, i: (0, ctx_blk0 + b))],
        out_specs=pl.BlockSpec((tq, vw), lambda b, i: (b * nq + i, 0)),
        compiler_params=_params(nbytes, 2),
        name="mla_attention",
    )(q, k, k, vt, vt)


def _rope_table(seq, pad_rows):
    t = jnp.arange(seq)
    row = (t // GRID_W).astype(F32)
    col = (t % GRID_W).astype(F32)
    d_axis = A_HEAD_DIM // 2
    inv_freq = ROPE_BASE ** (-jnp.arange(0, d_axis, 2, dtype=F32) / d_axis)
    ang_r = row[:, None] * inv_freq
    ang_c = col[:, None] * inv_freq
    ang = jnp.concatenate([ang_r, ang_r, ang_c, ang_c], axis=-1)
    cos, sin = jnp.cos(ang), jnp.sin(ang)
    first_half = (jnp.arange(A_HEAD_DIM) % 32) < 16
    s_next = jnp.where(first_half, -sin, 0.0)
    s_prev = jnp.where(first_half, 0.0, sin)
    tab = jnp.concatenate([jnp.tile(cos, (1, 2)), jnp.tile(s_next, (1, 2)), jnp.tile(s_prev, (1, 2))], axis=-1)
    ident = jnp.concatenate([jnp.ones((pad_rows, 128), F32), jnp.zeros((pad_rows, 256), F32)], axis=-1)
    return jnp.concatenate([tab, ident], axis=0)


def kernel(x, c, ctx, c_ctx, w_mod, b_mod, g_norm, ffn_w_in, ffn_w_out, ab_w_in, a_sink, b_w_dw, b_b_dw,
           b_ln_g, b_ln_b, ab_w_out, c_w_dq, c_g_q, c_w_uq, c_w_dkv, c_g_kv, c_w_uk, c_w_uv, c_w_o, g_final):
    batch, seq, d = x.shape
    ctx_len = ctx.shape[1]
    depth = w_mod.shape[0]
    d_ff = ffn_w_out.shape[2]
    n_lat = batch * seq
    rows = n_lat + batch * ctx_len
    tm = min(512, seq)
    tm_conv = min(256, ctx_len)
    tq = min(2 * V7X_MXU_DIM, seq)
    ck = V7X_MXU_DIM
    assert seq % tm == 0 and (batch * ctx_len) % tm == 0 and seq % A_BLOCK == 0 and seq % GRID_W == 0
    assert ctx_len % tm_conv == 0 and seq % tm_conv == 0 and n_lat % ctx_len == 0 and d_ff % ck == 0
    assert depth % 2 == 0 and depth <= 2

    r_mod = -(-(batch + 1) // 16) * 16
    cc = jnp.concatenate([c, c_ctx[None, :], jnp.zeros((r_mod - batch - 1, d), F32)], axis=0)
    mod_all = _modulation(cc, w_mod, b_mod)[:, :batch + 1].reshape(depth, batch + 1, N_MOD, d)

    rope_tab = _rope_table(seq, tm)
    lane_masks = (jnp.arange(128)[None, :] // 64 == jnp.arange(2)[:, None]).astype(BF16)
    t = (x.reshape(n_lat, d), ctx.reshape(batch * ctx_len, d))
    ffn_win, ffn_wout = ffn_w_in, ffn_w_out

    for i in range(depth):
        last = i == depth - 1
        j = i // 2
        mod = mod_all[i]
        g3 = g_norm[i]
        t = _ffn(t, mod, g3, ffn_win, ffn_wout, (i, 0), rows=rows, tm=tm, seq=seq, n_lat_rows=n_lat,
                 sub=0, ck=ck)
        if i % 2 == 0:
            w = ab_w_in[j]
            iq, ik = A_HEADS * A_HEAD_DIM, A_KV_HEADS * A_HEAD_DIM
            wk, wv = w[:, iq:iq + ik], w[:, iq + ik:iq + 2 * ik]
            swap = lambda m: jnp.concatenate([m[:, A_HEAD_DIM:], m[:, :A_HEAD_DIM]], axis=1)
            w_all = jnp.concatenate([w[:, :iq], wk, swap(wk), wv, swap(wv), w[:, iq + 2 * ik:]], axis=1).astype(BF16)
            q, kv, y = _proj0(t, mod, g3, w_all, rope_tab, tm=tm, seq=seq, n_lat_rows=n_lat)
            a = _gqa(q, kv, a_sink[j].astype(F32), lane_masks, batch=batch, seq=seq, ctx=ctx_len)
            bx = _conformer_conv(y, b_w_dw[j], b_b_dw[j], b_ln_g[j], b_ln_b[j],
                                 tm=tm_conv, seq=seq, ctx=ctx_len, n_lat_rows=n_lat)
            wo = ab_w_out[j].astype(BF16)
            attn, wos = (a, bx), (wo[:iq], wo[iq:])
        else:
            hp = C_HEAD_PAD
            wdkv = c_w_dkv[j]
            w1 = jnp.concatenate([c_w_dq[j], wdkv, jnp.zeros((d, 128 - C_ROPE), F32)], axis=1).astype(BF16)
            wuq = c_w_uq[j].reshape(C_Q_LORA, C_HEADS, C_NOPE + C_ROPE)
            wuq = jnp.pad(wuq, ((0, 0), (0, 0), (0, hp - C_NOPE - C_ROPE))).reshape(C_Q_LORA, C_HEADS * hp)
            q, k, vt = _proj1(t, mod, g3, w1, c_g_q[j].reshape(1, -1), c_g_kv[j].reshape(1, -1),
                              wuq.astype(BF16), c_w_uk[j].astype(BF16), c_w_uv[j].T.astype(BF16), rope_tab,
                              tm=tm, seq=seq, n_lat_rows=n_lat)
            a = _mla_attention(q, k, vt, batch=batch, seq=seq, ctx=ctx_len, tq=tq)
            attn, wos = (a,), (c_w_o[j].astype(BF16),)
        t = _ffn(t, mod, g3, ffn_win, ffn_wout, (i, 1), rows=n_lat if last else rows, tm=tm, seq=seq,
                 n_lat_rows=n_lat, sub=2, ck=ck, attn=attn, wo=wos, g_final=g_final if last else None)
    return t.reshape(batch, seq, d)
```

```python
import functools
import math

import jax
import jax.numpy as jnp
from jax import lax
from jax.experimental import pallas as pl
from jax.experimental.pallas import tpu as pltpu

F32 = jnp.float32
BF16 = jnp.bfloat16

GRID_W = 64
ROPE_BASE = 10000.0
NORM_EPS = 1e-6
LN_EPS = 1e-5
NEG_INF = -1e30
LOG2E = math.log2(math.e)
N_MOD = 9
FFN_RES = 0.5
A_HEADS = 8
A_KV_HEADS = 2
A_HEAD_DIM = 64
A_WINDOW = 128
A_BLOCK = 128
B_CH = 512
B_KERNEL = 31
C_HEADS = 8
C_Q_LORA = 256
C_KV_LORA = 256
C_NOPE = 128
C_ROPE = 64
C_V = 128

V7X_LANES = 128
V7X_SUBLANES = 8
V7X_MXU_DIM = 256
V7X_VMEM_USABLE = 56 * 1024 * 1024

C_HEAD_PAD = V7X_MXU_DIM
CONV_HALO = 16
GQA_BLOCKS = 8


def _vmem_limit(nbytes):
    return int(min(V7X_VMEM_USABLE, max(16 * 1024 * 1024, nbytes * 3 // 2)))


def _params(nbytes, ngrid=1):
    return pltpu.CompilerParams(dimension_semantics=("arbitrary",) * ngrid,
                                vmem_limit_bytes=_vmem_limit(nbytes))


def _rms(x, g):
    return x * lax.rsqrt(jnp.mean(x * x, axis=-1, keepdims=True) + NORM_EPS) * g


def _resident(shape, lead=()):
    nd = len(shape) - len(lead)
    return pl.BlockSpec((None,) * len(lead) + tuple(shape[len(lead):]),
                        lambda *_: tuple(lead) + (0,) * nd, pipeline_mode=pl.Buffered(1))


def _mod_kernel(c_ref, w_ref, b_ref, o_ref):
    c = c_ref[...]
    a = (c * jax.nn.sigmoid(c)).astype(BF16)
    o_ref[0] = jnp.dot(a, w_ref[0].astype(BF16), preferred_element_type=F32) + b_ref[0]


def _modulation(cc, w_mod, b_mod):
    depth, d, n = w_mod.shape
    r = cc.shape[0]
    tn = n // N_MOD
    nbytes = 2 * (d * tn * 4) + d * tn * 2 + 4 * r * (d + 2 * tn) * 4
    return pl.pallas_call(
        _mod_kernel,
        out_shape=jax.ShapeDtypeStruct((depth, r, n), F32),
        grid=(depth, n // tn),
        in_specs=[pl.BlockSpec((r, d), lambda l, j: (0, 0)),
                  pl.BlockSpec((1, d, tn), lambda l, j: (l, 0, j)),
                  pl.BlockSpec((1, 1, tn), lambda l, j: (l, 0, j))],
        out_specs=pl.BlockSpec((1, r, tn), lambda l, j: (l, 0, j)),
        compiler_params=_params(nbytes, 2),
        name="adaln_modulation",
    )(cc, w_mod, b_mod.reshape(depth, 1, n))


def _row_specs(op, tm, n_lat_tiles):
    if not isinstance(op, tuple):
        return [pl.BlockSpec((tm, op.shape[1]), lambda i: (i, 0))]
    lat, cx = op
    return [pl.BlockSpec((tm, lat.shape[1]), lambda i: (jnp.minimum(i, n_lat_tiles - 1), 0)),
            pl.BlockSpec((tm, cx.shape[1]), lambda i: (jnp.maximum(i - n_lat_tiles, 0), 0))]


def _row_load(refs, is_lat):
    if len(refs) == 1:
        return refs[0][...]
    return jnp.where(is_lat, refs[0][...], refs[1][...])


def _ffn_kernel(*refs, arity, sub, final_norm, ck, n_lat_tiles):
    refs = list(refs)
    o_ref = refs.pop()
    take = lambda n: [refs.pop(0) for _ in range(n)]
    x_refs = take(arity[0])
    mod_ref, g_ref = take(2)
    a_refs = [take(n) for n in arity[1:]]
    wo_refs = take(len(arity) - 1)
    win_ref, wout_ref = take(2)
    gf_ref = refs.pop(0) if final_norm else None
    is_lat = pl.program_id(0) < n_lat_tiles

    x = _row_load(x_refs, is_lat)
    mod = mod_ref[0]
    if a_refs:
        ox = None
        for ar, wo_ref in zip(a_refs, wo_refs):
            part = jnp.dot(_row_load(ar, is_lat), wo_ref[...], preferred_element_type=F32)
            ox = part if ox is None else ox + part
        x = x + mod[5:6] * ox
    shift, scale, gate = mod[3 * sub:3 * sub + 1], mod[3 * sub + 1:3 * sub + 2], mod[3 * sub + 2:3 * sub + 3]
    h = (_rms(x, g_ref[sub:sub + 1] * (1.0 + scale)) + shift).astype(BF16)
    d_ff = wout_ref.shape[0]
    acc = None
    for c in range(d_ff // ck):
        g = jnp.dot(h, win_ref[:, c * ck:(c + 1) * ck].astype(BF16), preferred_element_type=F32)
        u = jnp.dot(h, win_ref[:, d_ff + c * ck:d_ff + (c + 1) * ck].astype(BF16), preferred_element_type=F32)
        act = (g * jax.nn.sigmoid(g) * u).astype(BF16)
        part = jnp.dot(act, wout_ref[c * ck:(c + 1) * ck, :].astype(BF16), preferred_element_type=F32)
        acc = part if acc is None else acc + part
    y = x + (FFN_RES * gate) * acc
    if final_norm:
        y = _rms(y, gf_ref[...])
    o_ref[...] = y


def _ffn(x, mod, g3, win, wout, widx, *, rows, tm, seq, n_lat_rows, sub, ck, attn=(), wo=(), g_final=None):
    d = win.shape[-2]
    nb = mod.shape[0] - 1
    n_lat_tiles = n_lat_rows // tm
    arity = tuple(2 if isinstance(o, tuple) else 1 for o in (x, *attn))
    flat = lambda o: list(o) if isinstance(o, tuple) else [o]
    in_specs = _row_specs(x, tm, n_lat_tiles)
    in_specs += [pl.BlockSpec((1, N_MOD, d), lambda i: (jnp.minimum(i * tm // seq, nb), 0, 0)),
                 _resident(g3.shape)]
    for a in attn:
        in_specs += _row_specs(a, tm, n_lat_tiles)
    in_specs += [_resident(w.shape) for w in wo]
    in_specs += [_resident(win.shape, widx), _resident(wout.shape, widx)]
    args = [*flat(x), mod, g3, *[m for a in attn for m in flat(a)], *wo, win, wout]
    if g_final is not None:
        in_specs.append(_resident((1, d)))
        args.append(g_final.reshape(1, d))
    weights = (win.dtype.itemsize * (math.prod(win.shape[-2:]) + math.prod(wout.shape[-2:]))
               + 2 * sum(w.size for w in wo))
    tiles = 2 * tm * (arity[0] * d * 4 + d * 4 + sum(n * flat(a)[0].shape[1] for n, a in zip(arity[1:], attn)) * 2)
    temps = tm * (d * 4 * 3 + d * 2 + 2 * ck * 4 * 2 + ck * 2)
    kern = functools.partial(_ffn_kernel, arity=arity, sub=sub, final_norm=g_final is not None,
                             ck=ck, n_lat_tiles=n_lat_tiles)
    return pl.pallas_call(
        kern,
        out_shape=jax.ShapeDtypeStruct((rows, d), F32),
        grid=(rows // tm,),
        in_specs=in_specs,
        out_specs=pl.BlockSpec((tm, d), lambda i: (i, 0)),
        compiler_params=_params(weights + tiles + temps),
        name=f"ffn_sub{sub}" + ("_mix" if attn else "") + ("_final" if g_final is not None else ""),
    )(*args)


def _rope128(v, cos, sin_next, sin_prev):
    return (v * cos + pltpu.roll(v, V7X_LANES - 16, 1) * sin_next + pltpu.roll(v, 16, 1) * sin_prev)


def _proj0_kernel(x_ref, mod_ref, g_ref, w_ref, rope_ref, q_ref, kv_ref, y_ref, *, q_scale):
    mod = mod_ref[0]
    xm = (_rms(x_ref[...], g_ref[1:2] * (1.0 + mod[4:5])) + mod[3:4]).astype(BF16)
    nq = A_HEADS * A_HEAD_DIM // V7X_LANES
    base = nq * 128
    p_qk = jnp.dot(xm, w_ref[:, :base + 256], preferred_element_type=F32)
    p_ag = jnp.dot(xm, w_ref[:, base + 512:], preferred_element_type=F32)
    cos, s_next, s_prev = rope_ref[:, 0:128], rope_ref[:, 128:256], rope_ref[:, 256:384]
    for j in range(nq):
        blk = p_qk[:, j * 128:(j + 1) * 128]
        q_ref[:, j * 128:(j + 1) * 128] = (_rope128(blk, cos, s_next, s_prev) * q_scale).astype(BF16)
    for j in range(2):
        blk = p_qk[:, base + j * 128:base + (j + 1) * 128]
        kv_ref[:, j * 128:(j + 1) * 128] = _rope128(blk, cos, s_next, s_prev).astype(BF16)
    p_v = jnp.dot(xm, w_ref[:, base + 256:base + 512], preferred_element_type=F32)
    y_ref[...] = p_ag[:, :B_CH] * jax.nn.sigmoid(p_ag[:, B_CH:])
    kv_ref[:, 256:512] = p_v.astype(BF16)


def _proj0(x, mod, g3, w, rope_tab, *, tm, seq, n_lat_rows):
    rows, d = x.shape
    nb = mod.shape[0] - 1
    n_lat_tiles = n_lat_rows // tm
    tiles_per_seq = seq // tm
    rope_idx = lambda i: (jnp.where(i < n_lat_tiles, i % tiles_per_seq, tiles_per_seq), 0)
    nw = w.shape[1]
    nbytes = 2 * w.size + 2 * tm * (d * 4 + 384 * 4 + 1024 * 2 + 512 * 4) + tm * (nw * 4 + d * 6)
    return pl.pallas_call(
        functools.partial(_proj0_kernel, q_scale=A_HEAD_DIM ** -0.5 * LOG2E),
        out_shape=(jax.ShapeDtypeStruct((rows, 512), BF16),
                   jax.ShapeDtypeStruct((rows, 512), BF16),
                   jax.ShapeDtypeStruct((rows, B_CH), F32)),
        grid=(rows // tm,),
        in_specs=[pl.BlockSpec((tm, d), lambda i: (i, 0)),
                  pl.BlockSpec((1, N_MOD, d), lambda i: (jnp.minimum(i * tm // seq, nb), 0, 0)),
                  _resident(g3.shape), _resident(w.shape),
                  pl.BlockSpec((tm, 384), rope_idx)],
        out_specs=(pl.BlockSpec((tm, 512), lambda i: (i, 0)),
                   pl.BlockSpec((tm, 512), lambda i: (i, 0)),
                   pl.BlockSpec((tm, B_CH), lambda i: (i, 0))),
        compiler_params=_params(nbytes),
        name="ab_in_proj",
    )(x, mod, g3, w, rope_tab)


def _gqa_core(q_ref, kv_blocks, groups, lane_ref, sink_ref, o_ref):
    m_lo, m_hi = lane_ref[0:1, :], lane_ref[1:2, :]
    f_lo, f_hi = m_lo.astype(F32), m_hi.astype(F32)
    k_var, v_var = [], []
    for blk in kv_blocks:
        k, k_sw, v, v_sw = blk[:, 0:128], blk[:, 128:256], blk[:, 256:384], blk[:, 384:512]
        k_var.append(((k * m_lo, k_sw * m_hi), (k_sw * m_lo, k * m_hi)))
        v_var.append(((v * m_lo, v_sw * m_hi), (v_sw * m_lo, v * m_hi)))
    chains = [(g, hk) for g in groups for hk in range(A_KV_HEADS)]

    def scores(group, hk):
        row0, r, ids, _ = group
        qs = jnp.concatenate([q_ref[row0:row0 + r, (2 * hk) * 128:(2 * hk + 1) * 128],
                              q_ref[row0:row0 + r, (2 * hk + 1) * 128:(2 * hk + 2) * 128]], axis=0)
        kcat = jnp.concatenate([k_var[b][hk][0] for b in ids] + [k_var[b][hk][1] for b in ids], axis=0)
        return lax.dot_general(qs, kcat, (((1,), (1,)), ((), ())), preferred_element_type=F32)

    all_scores = [scores(g, hk) for g, hk in chains]
    for (group, hk), s in zip(chains, all_scores):
        row0, r, ids, mask = group
        nk = s.shape[1] // 2
        vcat = jnp.concatenate([v_var[b][hk][0] for b in ids] + [v_var[b][hk][1] for b in ids], axis=0)
        ps, rinv = [], []
        for j in range(2):
            pj, rj = [], []
            for par in range(2):
                sj = s[j * r:(j + 1) * r, par * nk:(par + 1) * nk]
                if mask is not None:
                    sj = jnp.where(mask, sj, NEG_INF)
                sink = sink_ref[4 * hk + 2 * j + par] * LOG2E
                m = jnp.maximum(jnp.max(sj, axis=-1, keepdims=True), sink)
                e = jnp.exp2(sj - m)
                l = jnp.sum(e, axis=-1, keepdims=True) + jnp.exp2(sink - m)
                pj.append(e.astype(BF16))
                rj.append(1.0 / l)
            ps.append(jnp.concatenate(pj, axis=1))
            rinv.append(rj[0] * f_lo + rj[1] * f_hi)
        p = jnp.concatenate(ps, axis=0)
        o = jnp.dot(p, vcat, preferred_element_type=F32)
        for j in range(2):
            blk = 2 * hk + j
            o_ref[row0:row0 + r, blk * 128:(blk + 1) * 128] = (o[j * r:(j + 1) * r] * rinv[j]).astype(BF16)


def _gqa_window_kernel(sink_ref, q_ref, kvp_ref, kvc_ref, kvn_ref, kvx_ref, lane_ref, o_ref, *, nstep):
    n = pl.program_id(1)
    nb = q_ref.shape[0] // A_BLOCK
    ctx = kvx_ref.shape[0]
    nk = 3 * A_BLOCK + ctx
    i = lax.broadcasted_iota(jnp.int32, (A_BLOCK, nk), 0)
    j = lax.broadcasted_iota(jnp.int32, (A_BLOCK, nk), 1)
    band = (j >= i) & (j <= i + 2 * A_WINDOW)
    is_ctx = j >= 3 * A_BLOCK
    lo = jnp.where(n > 0, 0, A_BLOCK)
    hi = jnp.where(n < nstep - 1, 3 * A_BLOCK, 2 * A_BLOCK)
    mask_mid = band | is_ctx
    cur = kvc_ref[...]
    blocks = [kvp_ref[...]] + [cur[g * A_BLOCK:(g + 1) * A_BLOCK] for g in range(nb)] + [kvn_ref[...], kvx_ref[...]]
    groups = []
    for g in range(nb):
        mask = mask_mid
        if g == 0:
            mask = (band & (j >= lo)) | is_ctx
        if g == nb - 1:
            mask = (band & (j < hi)) | is_ctx if g else (band & (j >= lo) & (j < hi)) | is_ctx
        groups.append((g * A_BLOCK, A_BLOCK, (g, g + 1, g + 2, nb + 2), mask))
    _gqa_core(q_ref, blocks, groups, lane_ref, sink_ref, o_ref)


def _gqa_ctx_kernel(sink_ref, q_ref, kvx_ref, lane_ref, o_ref):
    _gqa_core(q_ref, [kvx_ref[...]], [(0, q_ref.shape[0], (0,), None)], lane_ref, sink_ref, o_ref)


def _gqa(q, kv, sink, lane_masks, *, batch, seq, ctx):
    rows = q.shape[0]
    nblk = seq // A_BLOCK
    n_lat = batch * seq
    ctx_blk0 = n_lat // ctx
    smem = pl.BlockSpec(memory_space=pltpu.SMEM)
    lane_spec = lambda: pl.BlockSpec((2, 128), lambda *_: (0, 0))
    nb = GQA_BLOCKS if nblk % GQA_BLOCKS == 0 else 1
    nstep = nblk // nb
    one_blk = lambda f: pl.BlockSpec((A_BLOCK, 512), f)
    step_blk = lambda f: pl.BlockSpec((nb * A_BLOCK, 512), f)
    nbytes = 2 * ((2 * nb + 2) * A_BLOCK + ctx) * 512 * 2 + 5 * nb * (2 * A_BLOCK) * 2 * (3 * A_BLOCK + ctx) * 4
    o_lat = pl.pallas_call(
        functools.partial(_gqa_window_kernel, nstep=nstep),
        out_shape=jax.ShapeDtypeStruct((n_lat, 512), BF16),
        grid=(batch, nstep),
        in_specs=[smem,
                  step_blk(lambda b, n: (b * nstep + n, 0)),
                  one_blk(lambda b, n: (b * nblk + jnp.maximum(nb * n - 1, 0), 0)),
                  step_blk(lambda b, n: (b * nstep + n, 0)),
                  one_blk(lambda b, n: (b * nblk + jnp.minimum(nb * n + nb, nblk - 1), 0)),
                  pl.BlockSpec((ctx, 512), lambda b, n: (ctx_blk0 + b, 0)),
                  lane_spec()],
        out_specs=step_blk(lambda b, n: (b * nstep + n, 0)),
        compiler_params=_params(nbytes, 2),
        name="gqa_window",
    )(sink, q, kv, kv, kv, kv, lane_masks)
    nbytes_c = 2 * 3 * ctx * 512 * 2 + 6 * (2 * ctx) * (2 * ctx) * 4
    o_ctx = pl.pallas_call(
        _gqa_ctx_kernel,
        out_shape=jax.ShapeDtypeStruct((rows - n_lat, 512), BF16),
        grid=(batch,),
        in_specs=[smem,
                  pl.BlockSpec((ctx, 512), lambda b: (ctx_blk0 + b, 0)),
                  pl.BlockSpec((ctx, 512), lambda b: (ctx_blk0 + b, 0)),
                  lane_spec()],
        out_specs=pl.BlockSpec((ctx, 512), lambda b: (b, 0)),
        compiler_params=_params(nbytes_c, 1),
        name="gqa_context",
    )(sink, q, kv, lane_masks)
    return o_lat, o_ctx


def _conv_kernel(yp_ref, yc_ref, yn_ref, w_ref, b_ref, g_ref, beta_ref, o_ref, ext_ref, acc_ref, *,
                 n_lat_tiles, lat_tiles_per_seq, ctx_tiles_per_seq):
    i = pl.program_id(0)
    tm, ch = yc_ref.shape
    is_lat = i < n_lat_tiles
    pos = jnp.where(is_lat, i % lat_tiles_per_seq, (i - n_lat_tiles) % ctx_tiles_per_seq)
    last_pos = jnp.where(is_lat, lat_tiles_per_seq - 1, ctx_tiles_per_seq - 1)
    keep_prev = (pos > 0).astype(F32)
    keep_next = (pos < last_pos).astype(F32)
    n_ext = tm + 2 * CONV_HALO
    pitch = n_ext // V7X_SUBLANES
    half = B_KERNEL // 2
    pad = CONV_HALO
    zero = jnp.zeros((pad, V7X_LANES), F32)
    for cb in range(ch // V7X_LANES):
        cs = slice(cb * V7X_LANES, (cb + 1) * V7X_LANES)
        ext_ref[cb, 0:pad, :] = zero
        ext_ref[cb, pad:pad + CONV_HALO, :] = yp_ref[:, cs] * keep_prev
        ext_ref[cb, pad + CONV_HALO:pad + CONV_HALO + tm, :] = yc_ref[:, cs]
        ext_ref[cb, pad + CONV_HALO + tm:pad + n_ext, :] = yn_ref[:, cs] * keep_next
        ext_ref[cb, pad + n_ext:, :] = jnp.zeros((ext_ref.shape[1] - pad - n_ext, V7X_LANES), F32)
    for cb in range(ch // V7X_LANES):
        cs = slice(cb * V7X_LANES, (cb + 1) * V7X_LANES)
        for k in range(pitch):
            acc = None
            for t in range(B_KERNEL):
                term = w_ref[t:t + 1, cs] * ext_ref[cb, pl.ds(pad + k + t - half, V7X_SUBLANES, stride=pitch), :]
                acc = term if acc is None else acc + term
            acc_ref[cb, pl.ds(k, V7X_SUBLANES, stride=pitch), :] = acc + b_ref[:, cs]
    y = jnp.concatenate([acc_ref[cb, CONV_HALO:CONV_HALO + tm, :] for cb in range(ch // V7X_LANES)], axis=1)
    mu = jnp.mean(y, axis=-1, keepdims=True)
    yc = y - mu
    var = jnp.mean(yc * yc, axis=-1, keepdims=True)
    z = yc * lax.rsqrt(var + LN_EPS) * g_ref[...] + beta_ref[...]
    o_ref[...] = (z * jax.nn.sigmoid(z)).astype(BF16)


def _conformer_conv(y, w_dw, b_dw, ln_g, ln_b, *, tm, seq, ctx, n_lat_rows):
    rows, ch = y.shape
    hb = tm // CONV_HALO
    n_halo_blocks = rows // CONV_HALO
    n_slab = ch // V7X_LANES
    n_ext = tm + 2 * CONV_HALO
    pitch = n_ext // V7X_SUBLANES
    assert n_ext % V7X_SUBLANES == 0 and pitch % 8 != 0 and B_KERNEL // 2 < CONV_HALO
    ext_rows = -(-(CONV_HALO + pitch + B_KERNEL // 2 + (V7X_SUBLANES - 1) * pitch) // V7X_SUBLANES) * V7X_SUBLANES
    kern = functools.partial(_conv_kernel, n_lat_tiles=n_lat_rows // tm,
                             lat_tiles_per_seq=seq // tm, ctx_tiles_per_seq=ctx // tm)
    nbytes = 2 * (tm + 2 * CONV_HALO) * ch * 4 + 2 * tm * ch * 2 + (2 * tm + 2 * CONV_HALO) * ch * 4 + 6 * tm * ch * 4
    vec = lambda: pl.BlockSpec((1, ch), lambda i: (0, 0))
    return pl.pallas_call(
        kern,
        out_shape=jax.ShapeDtypeStruct((rows, ch), BF16),
        grid=(rows // tm,),
        in_specs=[pl.BlockSpec((CONV_HALO, ch), lambda i: (jnp.maximum(i * hb - 1, 0), 0)),
                  pl.BlockSpec((tm, ch), lambda i: (i, 0)),
                  pl.BlockSpec((CONV_HALO, ch), lambda i: (jnp.minimum((i + 1) * hb, n_halo_blocks - 1), 0)),
                  pl.BlockSpec((B_KERNEL, ch), lambda i: (0, 0)),
                  vec(), vec(), vec()],
        out_specs=pl.BlockSpec((tm, ch), lambda i: (i, 0)),
        scratch_shapes=[pltpu.VMEM((n_slab, ext_rows, V7X_LANES), F32), pltpu.VMEM((n_slab, n_ext, V7X_LANES), F32)],
        compiler_params=_params(nbytes),
        name="conformer_conv",
    )(y, y, y, w_dw, b_dw.reshape(1, ch), ln_g.reshape(1, ch), ln_b.reshape(1, ch))


def _proj1_kernel(x_ref, mod_ref, g_ref, w1_ref, gq_ref, gkv_ref, wq_ref, wk_ref, wvt_ref, rope_ref,
                  q_ref, k_ref, vt_ref, *, q_scale):
    mod = mod_ref[0]
    xm = (_rms(x_ref[...], g_ref[1:2] * (1.0 + mod[4:5])) + mod[3:4]).astype(BF16)
    c1 = jnp.dot(xm, w1_ref[...], preferred_element_type=F32)
    cos, s_next, s_prev = rope_ref[:, 0:128], rope_ref[:, 128:256], rope_ref[:, 256:384]
    cq = _rms(c1[:, :C_Q_LORA], gq_ref[...] * q_scale).astype(BF16)
    ckv = _rms(c1[:, C_Q_LORA:C_Q_LORA + C_KV_LORA], gkv_ref[...]).astype(BF16)
    kr = _rope128(c1[:, C_Q_LORA + C_KV_LORA:], cos, s_next, s_prev).astype(BF16)
    q = jnp.dot(cq, wq_ref[...], preferred_element_type=F32)
    kn = jnp.dot(ckv, wk_ref[...], preferred_element_type=F32)
    hp = C_HEAD_PAD
    for h in range(C_HEADS):
        q_ref[:, h * hp:h * hp + C_NOPE] = q[:, h * hp:h * hp + C_NOPE].astype(BF16)
        q_ref[:, h * hp + C_NOPE:(h + 1) * hp] = _rope128(
            q[:, h * hp + C_NOPE:(h + 1) * hp], cos, s_next, s_prev).astype(BF16)
        k_ref[:, h * hp:h * hp + C_NOPE] = kn[:, h * C_NOPE:(h + 1) * C_NOPE].astype(BF16)
        k_ref[:, h * hp + C_NOPE:(h + 1) * hp] = kr
    vt_ref[...] = lax.dot_general(wvt_ref[...], ckv, (((1,), (1,)), ((), ())),
                                  preferred_element_type=F32).astype(BF16)


def _proj1(x, mod, g3, w1, gq, gkv, wq, wk, wvt, rope_tab, *, tm, seq, n_lat_rows):
    rows, d = x.shape
    nb = mod.shape[0] - 1
    n_lat_tiles = n_lat_rows // tm
    tiles_per_seq = seq // tm
    rope_idx = lambda i: (jnp.where(i < n_lat_tiles, i % tiles_per_seq, tiles_per_seq), 0)
    qw = C_HEADS * C_HEAD_PAD
    vw = C_HEADS * C_V
    nbytes = (2 * (w1.size + wq.size + wk.size + wvt.size) + 2 * tm * (d * 4 + 384 * 4 + (2 * qw + vw) * 2)
              + tm * (w1.shape[1] + wq.shape[1] + wk.shape[1] + vw) * 4 + tm * d * 6)
    row = lambda w: pl.BlockSpec((tm, w), lambda i: (i, 0))
    return pl.pallas_call(
        functools.partial(_proj1_kernel, q_scale=(C_NOPE + C_ROPE) ** -0.5 * LOG2E),
        out_shape=(jax.ShapeDtypeStruct((rows, qw), BF16),
                   jax.ShapeDtypeStruct((rows, qw), BF16),
                   jax.ShapeDtypeStruct((C_HEADS * C_V, rows), BF16)),
        grid=(rows // tm,),
        in_specs=[row(d),
                  pl.BlockSpec((1, N_MOD, d), lambda i: (jnp.minimum(i * tm // seq, nb), 0, 0)),
                  _resident(g3.shape), _resident(w1.shape), _resident(gq.shape), _resident(gkv.shape),
                  _resident(wq.shape), _resident(wk.shape), _resident(wvt.shape),
                  pl.BlockSpec((tm, 384), rope_idx)],
        out_specs=(row(qw), row(qw), pl.BlockSpec((C_HEADS * C_V, tm), lambda i: (0, i))),
        compiler_params=_params(nbytes),
        name="mla_proj",
    )(x, mod, g3, w1, gq, gkv, wq, wk, wvt, rope_tab)


def _col_reduce(s, op, chains=8):
    r, n = s.shape
    s3 = s.reshape(r // V7X_SUBLANES, V7X_SUBLANES, n)
    chains = min(chains, s3.shape[0])
    step = s3.shape[0] // chains
    parts = []
    for g in range(chains):
        acc = s3[g * step]
        for i in range(g * step + 1, (g + 1) * step if g < chains - 1 else s3.shape[0]):
            acc = op(acc, s3[i])
        parts.append(acc)
    while len(parts) > 1:
        parts = [op(parts[i], parts[i + 1]) for i in range(0, len(parts) - 1, 2)] + (
            [parts[-1]] if len(parts) % 2 else [])
    return parts[0]


def _mla_kernel(q_ref, kl_ref, kc_ref, vtl_ref, vtc_ref, o_ref):
    hp = C_HEAD_PAD
    dn = (((1,), (1,)), ((), ()))

    def scores(h):
        qh = q_ref[:, h * hp:(h + 1) * hp]
        return (lax.dot_general(kl_ref[:, h * hp:(h + 1) * hp], qh, dn, preferred_element_type=F32),
                lax.dot_general(kc_ref[:, h * hp:(h + 1) * hp], qh, dn, preferred_element_type=F32))

    nxt = scores(0)
    for h in range(C_HEADS):
        s1, s2 = nxt
        if h + 1 < C_HEADS:
            nxt = scores(h + 1)
        m = jnp.max(jnp.maximum(_col_reduce(s1, jnp.maximum), _col_reduce(s2, jnp.maximum)),
                    axis=0, keepdims=True)
        e1 = jnp.exp2(s1 - m)
        e2 = jnp.exp2(s2 - m)
        l = jnp.sum(_col_reduce(e1, jnp.add) + _col_reduce(e2, jnp.add), axis=0, keepdims=True)
        ot = (jnp.dot(vtl_ref[h * C_V:(h + 1) * C_V, :], e1.astype(BF16), preferred_element_type=F32)
              + jnp.dot(vtc_ref[h * C_V:(h + 1) * C_V, :], e2.astype(BF16), preferred_element_type=F32))
        o_ref[:, h * C_V:(h + 1) * C_V] = (ot * (1.0 / l)).T.astype(BF16)


def _mla_attention(q, k, vt, *, batch, seq, ctx, tq):
    n_lat = batch * seq
    qw, vw, vpw = k.shape[1], C_HEADS * C_V, vt.shape[0]
    nq = seq // tq
    ctx_blk0 = n_lat // ctx
    nbytes = (2 * (seq + ctx) * (qw + vw) * 2 + 2 * tq * (qw + vw) * 2 + 4 * tq * (seq + ctx) * 4)
    return pl.pallas_call(
        _mla_kernel,
        out_shape=jax.ShapeDtypeStruct((n_lat, vw), BF16),
        grid=(batch, nq),
        in_specs=[pl.BlockSpec((tq, qw), lambda b, i: (b * nq + i, 0)),
                  pl.BlockSpec((seq, qw), lambda b, i: (b, 0)),
                  pl.BlockSpec((ctx, qw), lambda b, i: (ctx_blk0 + b, 0)),
                  pl.BlockSpec((vpw, seq), lambda b, i: (0, b)),
                  pl.BlockSpec((vpw, ctx), lambda b, i: (0, ctx_blk0 + b))],
        out_specs=pl.BlockSpec((tq, vw), lambda b, i: (b * nq + i, 0)),
        compiler_params=_params(nbytes, 2),
        name="mla_attention",
    )(q, k, k, vt, vt)


def _rope_table(seq, pad_rows):
    t = jnp.arange(seq)
    row = (t // GRID_W).astype(F32)
    col = (t % GRID_W).astype(F32)
    d_axis = A_HEAD_DIM // 2
    inv_freq = ROPE_BASE ** (-jnp.arange(0, d_axis, 2, dtype=F32) / d_axis)
    ang_r = row[:, None] * inv_freq
    ang_c = col[:, None] * inv_freq
    ang = jnp.concatenate([ang_r, ang_r, ang_c, ang_c], axis=-1)
    cos, sin = jnp.cos(ang), jnp.sin(ang)
    first_half = (jnp.arange(A_HEAD_DIM) % 32) < 16
    s_next = jnp.where(first_half, -sin, 0.0)
    s_prev = jnp.where(first_half, 0.0, sin)
    tab = jnp.concatenate([jnp.tile(cos, (1, 2)), jnp.tile(s_next, (1, 2)), jnp.tile(s_prev, (1, 2))], axis=-1)
    ident = jnp.concatenate([jnp.ones((pad_rows, 128), F32), jnp.zeros((pad_rows, 256), F32)], axis=-1)
    return jnp.concatenate([tab, ident], axis=0)


def kernel(x, c, ctx, c_ctx, w_mod, b_mod, g_norm, ffn_w_in, ffn_w_out, ab_w_in, a_sink, b_w_dw, b_b_dw,
           b_ln_g, b_ln_b, ab_w_out, c_w_dq, c_g_q, c_w_uq, c_w_dkv, c_g_kv, c_w_uk, c_w_uv, c_w_o, g_final):
    batch, seq, d = x.shape
    ctx_len = ctx.shape[1]
    depth = w_mod.shape[0]
    d_ff = ffn_w_out.shape[2]
    n_lat = batch * seq
    rows = n_lat + batch * ctx_len
    tm = min(512, seq)
    tm_conv = min(256, ctx_len)
    tq = min(2 * V7X_MXU_DIM, seq)
    ck = V7X_MXU_DIM
    assert seq % tm == 0 and (batch * ctx_len) % tm == 0 and seq % A_BLOCK == 0 and seq % GRID_W == 0
    assert ctx_len % tm_conv == 0 and seq % tm_conv == 0 and n_lat % ctx_len == 0 and d_ff % ck == 0
    assert depth % 2 == 0 and depth <= 2

    r_mod = -(-(batch + 1) // 16) * 16
    cc = jnp.concatenate([c, c_ctx[None, :], jnp.zeros((r_mod - batch - 1, d), F32)], axis=0)
    mod_all = _modulation(cc, w_mod, b_mod)[:, :batch + 1].reshape(depth, batch + 1, N_MOD, d)

    rope_tab = _rope_table(seq, tm)
    lane_masks = (jnp.arange(128)[None, :] // 64 == jnp.arange(2)[:, None]).astype(BF16)
    t = (x.reshape(n_lat, d), ctx.reshape(batch * ctx_len, d))
    ffn_win, ffn_wout = ffn_w_in, ffn_w_out

    for i in range(depth):
        last = i == depth - 1
        j = i // 2
        mod = mod_all[i]
        g3 = g_norm[i]
        t = _ffn(t, mod, g3, ffn_win, ffn_wout, (i, 0), rows=rows, tm=tm, seq=seq, n_lat_rows=n_lat,
                 sub=0, ck=ck)
        if i % 2 == 0:
            w = ab_w_in[j]
            iq, ik = A_HEADS * A_HEAD_DIM, A_KV_HEADS * A_HEAD_DIM
            wk, wv = w[:, iq:iq + ik], w[:, iq + ik:iq + 2 * ik]
            swap = lambda m: jnp.concatenate([m[:, A_HEAD_DIM:], m[:, :A_HEAD_DIM]], axis=1)
            w_all = jnp.concatenate([w[:, :iq], wk, swap(wk), wv, swap(wv), w[:, iq + 2 * ik:]], axis=1).astype(BF16)
            q, kv, y = _proj0(t, mod, g3, w_all, rope_tab, tm=tm, seq=seq, n_lat_rows=n_lat)
            a = _gqa(q, kv, a_sink[j].astype(F32), lane_masks, batch=batch, seq=seq, ctx=ctx_len)
            bx = _conformer_conv(y, b_w_dw[j], b_b_dw[j], b_ln_g[j], b_ln_b[j],
                                 tm=tm_conv, seq=seq, ctx=ctx_len, n_lat_rows=n_lat)
            wo = ab_w_out[j].astype(BF16)
            attn, wos = (a, bx), (wo[:iq], wo[iq:])
        else:
            hp = C_HEAD_PAD
            wdkv = c_w_dkv[j]
            w1 = jnp.concatenate([c_w_dq[j], wdkv, jnp.zeros((d, 128 - C_ROPE), F32)], axis=1).astype(BF16)
            wuq = c_w_uq[j].reshape(C_Q_LORA, C_HEADS, C_NOPE + C_ROPE)
            wuq = jnp.pad(wuq, ((0, 0), (0, 0), (0, hp - C_NOPE - C_ROPE))).reshape(C_Q_LORA, C_HEADS * hp)
            q, k, vt = _proj1(t, mod, g3, w1, c_g_q[j].reshape(1, -1), c_g_kv[j].reshape(1, -1),
                              wuq.astype(BF16), c_w_uk[j].astype(BF16), c_w_uv[j].T.astype(BF16), rope_tab,
                              tm=tm, seq=seq, n_lat_rows=n_lat)
            a = _mla_attention(q, k, vt, batch=batch, seq=seq, ctx=ctx_len, tq=tq)
            attn, wos = (a,), (c_w_o[j].astype(BF16),)
        t = _ffn(t, mod, g3, ffn_win, ffn_wout, (i, 1), rows=n_lat if last else rows, tm=tm, seq=seq,
                 n_lat_rows=n_lat, sub=2, ck=ck, attn=attn, wo=wos, g_final=g_final if last else None)
    return t.reshape(batch, seq, d)
```
